```python
import jax, jax.numpy as jnp
from jax import lax
import numpy as np

D_MODEL = 1024
BATCH = 8
SEQ = 2048
DEPTH = 1
DEC_BATCH = 128
DEC_SEQ = 1
PAST_LEN = 16384
PAGE_SIZE = 128

POOL_WIDTH = D_MODEL // 2
POOL_WINDOWS = (2, 4, 8, 16)
POOL_GROUPS = len(POOL_WINDOWS)
POOL_GROUP_DIM = POOL_WIDTH // POOL_GROUPS
POOL_OUT_GROUP_DIM = D_MODEL // POOL_GROUPS
POOL_BUF = max(POOL_WINDOWS) - 1
SSM_WIDTH = D_MODEL // 2
SSM_GROUP_DIM = 16
SSM_GROUPS = SSM_WIDTH // SSM_GROUP_DIM
SSM_STATE = 64
DT_MIN = 1e-3
DT_MAX = 1e-1
D_FF = 4 * D_MODEL
PLE_DIM = 256
EPS = 1e-6
IN_WIDTH = POOL_WIDTH + SSM_WIDTH + 2 * D_MODEL

kernel_name = 'pool_s5_gated_hybrid_step'


def rmsnorm(x, g):
    xf = x.astype(jnp.float32)
    y = xf * lax.rsqrt(jnp.mean(xf * xf, axis=-1, keepdims=True) + EPS)
    return (y * g.astype(jnp.float32)).astype(x.dtype)


def pool_mixer(u, buf, pos_start, w_pool, pool_scale):
    n, l, _ = u.shape
    lb = buf.shape[1]
    ext = jnp.concatenate([buf, u], axis=1).astype(jnp.float32)
    l_ext = ext.shape[1]
    pos = pos_start - lb + jnp.arange(l_ext, dtype=jnp.int32)
    groups = ext.reshape(n, l_ext, POOL_GROUPS, POOL_GROUP_DIM)
    outs = []
    for gi, w in enumerate(POOL_WINDOWS):
        xg = groups[:, :, gi]
        s = jnp.cumsum(xg, axis=1)
        s_lag = jnp.pad(s, ((0, 0), (w, 0), (0, 0)))[:, :l_ext]
        count = jnp.minimum(pos + 1, w).astype(jnp.float32)
        mean = (s - s_lag) / count[None, :, None]
        outs.append(mean - xg)
    pooled = jnp.stack(outs, axis=2)[:, lb:]
    y = jnp.einsum('nlgc,gcd->nlgd', pooled, w_pool.astype(jnp.float32))
    y = y.reshape(n, l, D_MODEL) * pool_scale.astype(jnp.float32)
    new_buf = ext[:, -POOL_BUF:].astype(buf.dtype)
    return y.astype(u.dtype), new_buf


def _cplx_affine_combine(e1, e2):
    a1r, a1i, b1r, b1i = e1
    a2r, a2i, b2r, b2i = e2
    ar = a2r * a1r - a2i * a1i
    ai = a2r * a1i + a2i * a1r
    br = a2r * b1r - a2i * b1i + b2r
    bi = a2r * b1i + a2i * b1r + b2i
    return (ar, ai, br, bi)


def s5_mixer(u, h0_re, h0_im, lam_re, lam_im, log_dt, b_re, b_im, c_re, c_im, d_skip):
    n, l, _ = u.shape
    uf = u.astype(jnp.float32).reshape(n, l, SSM_GROUPS, SSM_GROUP_DIM)
    dt = jnp.exp(log_dt.astype(jnp.float32))[:, None]
    lr = lam_re.astype(jnp.float32)
    li = lam_im.astype(jnp.float32)
    mag = jnp.exp(lr * dt)
    ang = li * dt
    abar_re = mag * jnp.cos(ang)
    abar_im = mag * jnp.sin(ang)
    den = lr * lr + li * li
    nr = abar_re - 1.0
    ni = abar_im
    k_re = ((nr * lr + ni * li) / den)[:, :, None]
    k_im = ((ni * lr - nr * li) / den)[:, :, None]
    br = b_re.astype(jnp.float32)
    bi = b_im.astype(jnp.float32)
    bbar_re = k_re * br - k_im * bi
    bbar_im = k_re * bi + k_im * br
    bu_re = jnp.einsum('nlgh,gph->nlgp', uf, bbar_re)
    bu_im = jnp.einsum('nlgh,gph->nlgp', uf, bbar_im)
    h0r = h0_re.astype(jnp.float32)
    h0i = h0_im.astype(jnp.float32)
    bu_re = bu_re.at[:, 0].add(abar_re * h0r - abar_im * h0i)
    bu_im = bu_im.at[:, 0].add(abar_re * h0i + abar_im * h0r)
    a_re = jnp.broadcast_to(abar_re, bu_re.shape)
    a_im = jnp.broadcast_to(abar_im, bu_im.shape)
    _, _, hr, hi = lax.associative_scan(_cplx_affine_combine, (a_re, a_im, bu_re, bu_im), axis=1)
    y = (jnp.einsum('nlgp,ghp->nlgh', hr, c_re.astype(jnp.float32))
         - jnp.einsum('nlgp,ghp->nlgh', hi, c_im.astype(jnp.float32))
         + d_skip.astype(jnp.float32).reshape(SSM_GROUPS, SSM_GROUP_DIM) * uf)
    return (y.reshape(n, l, SSM_WIDTH).astype(u.dtype),
            hr[:, -1].astype(h0_re.dtype), hi[:, -1].astype(h0_im.dtype))


def layer(x, p_l, pool_buf, h_re, h_im, pos_start,
          g_mix, w_in, w_pool, pool_scale, lam_re, lam_im, log_dt, b_re, b_im, c_re, c_im,
          d_skip, w_glu_v, w_glu_g, w_out, g_ff, w_ff1, w_ff2, g_ple, w_ple, w_ple_gate):
    h = rmsnorm(x, g_mix)
    z = h @ w_in
    s1 = POOL_WIDTH
    s2 = s1 + SSM_WIDTH
    s3 = s2 + D_MODEL
    u_a, u_b, gate_a, gate_b = z[..., :s1], z[..., s1:s2], z[..., s2:s3], z[..., s3:]
    branch_a, new_buf = pool_mixer(u_a, pool_buf, pos_start, w_pool, pool_scale)
    s, new_re, new_im = s5_mixer(u_b, h_re, h_im, lam_re, lam_im, log_dt, b_re, b_im,
                                 c_re, c_im, d_skip)
    gl = jax.nn.gelu(s)
    branch_b = (gl @ w_glu_v) * jax.nn.sigmoid(gl @ w_glu_g)
    merged = jax.nn.sigmoid(gate_a) * branch_a + jax.nn.sigmoid(gate_b) * branch_b
    x = x + merged @ w_out
    h2 = rmsnorm(x, g_ff)
    x = x + jnp.square(jax.nn.relu(h2 @ w_ff1)) @ w_ff2
    h3 = rmsnorm(x, g_ple)
    x = x + (p_l @ w_ple) * jax.nn.sigmoid(h3 @ w_ple_gate)
    return x, new_buf, new_re, new_im


def setup_inputs(seed: int = 0) -> dict:
    key = jax.random.key(seed)
    ks = jax.random.split(key, 32)
    f32 = jnp.float32
    nrm = lambda k, shape, s: jax.random.normal(k, shape, f32) * s
    gain = lambda k: 1.0 + 0.02 * jax.random.normal(k, (DEPTH, D_MODEL), f32)
    lam_im = (jnp.pi * jnp.arange(SSM_STATE, dtype=f32))[None, None, :] + nrm(ks[8], (DEPTH, SSM_GROUPS, SSM_STATE), 1e-3)
    return {
        'x_prompt': nrm(ks[0], (BATCH, SEQ, D_MODEL), 1.0),
        'x_sample': nrm(ks[1], (DEC_BATCH, DEC_SEQ, D_MODEL), 1.0),
        'p_prompt': nrm(ks[2], (DEPTH, BATCH, SEQ, PLE_DIM), 1.0),
        'p_sample': nrm(ks[3], (DEPTH, DEC_BATCH, DEC_SEQ, PLE_DIM), 1.0),
        'state_pool': nrm(ks[4], (DEPTH, DEC_BATCH, POOL_BUF, POOL_WIDTH), 1.0),
        'state_ssm_re': nrm(ks[5], (DEPTH, DEC_BATCH, SSM_GROUPS, SSM_STATE), 0.1),
        'state_ssm_im': nrm(ks[6], (DEPTH, DEC_BATCH, SSM_GROUPS, SSM_STATE), 0.1),
        'g_mix': gain(ks[7]),
        'w_in': nrm(ks[9], (DEPTH, D_MODEL, IN_WIDTH), D_MODEL ** -0.5),
        'w_pool': nrm(ks[10], (DEPTH, POOL_GROUPS, POOL_GROUP_DIM, POOL_OUT_GROUP_DIM), POOL_GROUP_DIM ** -0.5),
        'pool_scale': 1.0 + 0.1 * jax.random.normal(ks[11], (DEPTH, D_MODEL), f32),
        'lam_re': -0.5 + nrm(ks[12], (DEPTH, SSM_GROUPS, SSM_STATE), 1e-3),
        'lam_im': lam_im,
        'log_dt': jax.random.uniform(ks[13], (DEPTH, SSM_GROUPS), f32, float(np.log(DT_MIN)), float(np.log(DT_MAX))),
        'b_re': nrm(ks[14], (DEPTH, SSM_GROUPS, SSM_STATE, SSM_GROUP_DIM), (2 * SSM_GROUP_DIM) ** -0.5),
        'b_im': nrm(ks[15], (DEPTH, SSM_GROUPS, SSM_STATE, SSM_GROUP_DIM), (2 * SSM_GROUP_DIM) ** -0.5),
        'c_re': nrm(ks[16], (DEPTH, SSM_GROUPS, SSM_GROUP_DIM, SSM_STATE), SSM_STATE ** -0.5),
        'c_im': nrm(ks[17], (DEPTH, SSM_GROUPS, SSM_GROUP_DIM, SSM_STATE), SSM_STATE ** -0.5),
        'd_skip': nrm(ks[18], (DEPTH, SSM_WIDTH), 1.0),
        'w_glu_v': nrm(ks[19], (DEPTH, SSM_WIDTH, D_MODEL), SSM_WIDTH ** -0.5),
        'w_glu_g': nrm(ks[20], (DEPTH, SSM_WIDTH, D_MODEL), SSM_WIDTH ** -0.5),
        'w_out': nrm(ks[21], (DEPTH, D_MODEL, D_MODEL), D_MODEL ** -0.5),
        'g_ff': gain(ks[22]),
        'w_ff1': nrm(ks[23], (DEPTH, D_MODEL, D_FF), D_MODEL ** -0.5),
        'w_ff2': nrm(ks[24], (DEPTH, D_FF, D_MODEL), D_FF ** -0.5),
        'g_ple': gain(ks[25]),
        'w_ple': nrm(ks[26], (DEPTH, PLE_DIM, D_MODEL), PLE_DIM ** -0.5),
        'w_ple_gate': nrm(ks[27], (DEPTH, D_MODEL, D_MODEL), D_MODEL ** -0.5),
        'g_final': 1.0 + 0.02 * jax.random.normal(ks[28], (D_MODEL,), f32),
    }


def reference(x_prompt, x_sample, p_prompt, p_sample, state_pool, state_ssm_re, state_ssm_im,
              g_mix, w_in, w_pool, pool_scale, lam_re, lam_im, log_dt, b_re, b_im, c_re, c_im,
              d_skip, w_glu_v, w_glu_g, w_out, g_ff, w_ff1, w_ff2, g_ple, w_ple, w_ple_gate,
              g_final):
    yp = x_prompt
    ys = x_sample
    empty_buf = jnp.zeros((BATCH, 0, POOL_WIDTH), state_pool.dtype)
    zero_state = jnp.zeros((BATCH, SSM_GROUPS, SSM_STATE), state_ssm_re.dtype)
    pool_p, pool_s, re_p, im_p, re_s, im_s = [], [], [], [], [], []
    for l in range(DEPTH):
        lw = (g_mix[l], w_in[l], w_pool[l], pool_scale[l], lam_re[l], lam_im[l], log_dt[l],
              b_re[l], b_im[l], c_re[l], c_im[l], d_skip[l], w_glu_v[l], w_glu_g[l], w_out[l],
              g_ff[l], w_ff1[l], w_ff2[l], g_ple[l], w_ple[l], w_ple_gate[l])
        yp, bp, hrp, hip = layer(yp, p_prompt[l], empty_buf, zero_state, zero_state, 0, *lw)
        ys, bs, hrs, his = layer(ys, p_sample[l], state_pool[l], state_ssm_re[l],
                                 state_ssm_im[l], PAST_LEN, *lw)
        pool_p.append(bp)
        pool_s.append(bs)
        re_p.append(hrp)
        im_p.append(hip)
        re_s.append(hrs)
        im_s.append(his)
    y_prompt = rmsnorm(yp, g_final)
    y_sample = rmsnorm(ys, g_final)
    return (y_prompt, y_sample, jnp.stack(pool_p), jnp.stack(pool_s), jnp.stack(re_p),
            jnp.stack(im_p), jnp.stack(re_s), jnp.stack(im_s))
```

```python
import functools

import jax
import jax.numpy as jnp
import numpy as np
from jax import lax
from jax.experimental import pallas as pl
from jax.experimental.pallas import tpu as pltpu

D_MODEL = 1024
POOL_WINDOWS = (2, 4, 8, 16)
POOL_WIDTH = D_MODEL // 2
POOL_GROUP_DIM = POOL_WIDTH // len(POOL_WINDOWS)
POOL_OUT_GROUP_DIM = D_MODEL // len(POOL_WINDOWS)
POOL_BUF = max(POOL_WINDOWS) - 1
POOL_SLOTS = POOL_BUF + 1
SSM_WIDTH = D_MODEL // 2
SSM_GROUP_DIM = 16
SSM_GROUPS = SSM_WIDTH // SSM_GROUP_DIM
SSM_STATE = 64
STATE_LANES = 2 * SSM_GROUPS * SSM_STATE
HALF_LANES = STATE_LANES // 2
PART_LANES = HALF_LANES // 2
D_FF = 4 * D_MODEL
FF_CHUNK = 1024
PLE_DIM = 256
EPS = 1e-6
IN_WIDTH = POOL_WIDTH + SSM_WIDTH + 2 * D_MODEL
PAST_LEN = 16384
PROMPT_T_CHUNK = 64
FFN_BLOCK_ROWS = 512
SCAN_LANES = 512
VMEM_LIMIT_BYTES = 58 * 1024 * 1024

BF16 = jnp.bfloat16
F32 = jnp.float32


def _rmsnorm(x, g):
    return x * lax.rsqrt(jnp.mean(x * x, axis=-1, keepdims=True) + EPS) * g


def _dot(a, b):
    return jnp.dot(a, b, preferred_element_type=F32)


def _mixer_kernel(*refs, n_seq, t_chunk, permute, pos_start, has_state):
    it = iter(refs)
    x_ref = next(it)
    pool_in_ref = next(it) if has_state else None
    h0_ref = next(it) if has_state else None
    perm_ref = next(it) if permute else None
    perm_t_ref = next(it) if permute else None
    (g_mix_ref, w_in_ref, w_pool_ref, pool_scale_ref, abar_ref, bbar_ref, cmat_ref, d_skip_ref,
     w_glu_ref, w_out_ref) = (next(it) for _ in range(10))
    x1_ref, ua_out_ref, h_out_ref = (next(it) for _ in range(3))
    z_ref, ext_ref, hs_ref, hc_ref, ya_ref = (next(it) for _ in range(5))

    rows = n_seq * t_chunk
    step = pl.program_id(0)
    cur0 = POOL_SLOTS * n_seq

    @pl.when(step == 0)
    def _init():
        if has_state:
            ext_ref[0:cur0, :] = pool_in_ref[...]
            hc_ref[...] = h0_ref[...]
        else:
            ext_ref[0:cur0, :] = jnp.zeros((cur0, POOL_WIDTH), F32)
            hc_ref[...] = jnp.zeros((n_seq, STATE_LANES), F32)

    x = x_ref[...].reshape(rows, D_MODEL)
    h = _rmsnorm(x, g_mix_ref[...]).astype(BF16)
    if permute:
        h = _dot(perm_ref[...], h).astype(BF16)
    z_ref[...] = _dot(h, w_in_ref[...])

    ext_ref[cur0:cur0 + rows, :] = z_ref[:, 0:POOL_WIDTH]
    t_idx = lax.shift_right_logical(
        lax.broadcasted_iota(jnp.int32, (rows, POOL_GROUP_DIM), 0), n_seq.bit_length() - 1)
    pos = t_idx + (pos_start + step * t_chunk)
    for gi, w in enumerate(POOL_WINDOWS):
        c0 = gi * POOL_GROUP_DIM
        cur = ext_ref[cur0:cur0 + rows, c0:c0 + POOL_GROUP_DIM]
        acc = cur
        for k in range(1, w):
            r0 = cur0 - k * n_seq
            acc = acc + ext_ref[r0:r0 + rows, c0:c0 + POOL_GROUP_DIM]
        count = jnp.minimum(pos + 1, w).astype(F32)
        pooled = (acc / count - cur).astype(BF16)
        o0 = gi * POOL_OUT_GROUP_DIM
        ya_ref[:, o0:o0 + POOL_OUT_GROUP_DIM] = (
            _dot(pooled, w_pool_ref[gi]) * pool_scale_ref[:, o0:o0 + POOL_OUT_GROUP_DIM])
    ua_out_ref[...] = ext_ref[t_chunk * n_seq:(t_chunk + POOL_SLOTS) * n_seq, :]
    if t_chunk >= POOL_SLOTS:
        ext_ref[0:cur0, :] = ext_ref[t_chunk * n_seq:(t_chunk + POOL_SLOTS) * n_seq, :]

    u_b = z_ref[:, POOL_WIDTH:POOL_WIDTH + SSM_WIDTH]
    u_b16 = u_b.astype(BF16)
    half_in = SSM_WIDTH // 2
    for hf in range(2):
        hs_ref[:, hf * HALF_LANES:(hf + 1) * HALF_LANES] = _dot(
            u_b16[:, hf * half_in:(hf + 1) * half_in], bbar_ref[hf])

    for hf in range(2):
        for q in range(PART_LANES // SCAN_LANES):
            re0 = hf * HALF_LANES + q * SCAN_LANES
            im0 = re0 + PART_LANES
            a_re = abar_ref[:, re0:re0 + SCAN_LANES]
            a_im = abar_ref[:, im0:im0 + SCAN_LANES]
            if n_seq == 8:
                a_re = jnp.broadcast_to(a_re, (n_seq, SCAN_LANES))
                a_im = jnp.broadcast_to(a_im, (n_seq, SCAN_LANES))

            def scan_step(t, carry, re0=re0, im0=im0, a_re=a_re, a_im=a_im):
                h_re, h_im = carry
                r0 = t * n_seq if isinstance(t, int) else pl.multiple_of(t * n_seq, n_seq)
                n_re = a_re * h_re - a_im * h_im + hs_ref[pl.ds(r0, n_seq), re0:re0 + SCAN_LANES]
                n_im = a_re * h_im + a_im * h_re + hs_ref[pl.ds(r0, n_seq), im0:im0 + SCAN_LANES]
                hs_ref[pl.ds(r0, n_seq), re0:re0 + SCAN_LANES] = n_re
                hs_ref[pl.ds(r0, n_seq), im0:im0 + SCAN_LANES] = n_im
                return n_re, n_im

            carry = (hc_ref[:, re0:re0 + SCAN_LANES], hc_ref[:, im0:im0 + SCAN_LANES])
            if t_chunk == 1:
                carry = scan_step(0, carry)
            else:
                carry = lax.fori_loop(0, t_chunk, scan_step, carry, unroll=8)
            hc_ref[:, re0:re0 + SCAN_LANES] = carry[0]
            hc_ref[:, im0:im0 + SCAN_LANES] = carry[1]
    h_out_ref[...] = hc_ref[...]

    y_halves = [
        _dot(hs_ref[:, hf * HALF_LANES:(hf + 1) * HALF_LANES].astype(BF16), cmat_ref[hf])
        for hf in range(2)]
    s = jnp.concatenate(y_halves, axis=-1) + d_skip_ref[...] * u_b

    gl = jax.nn.gelu(s).astype(BF16)
    vg = _dot(gl, w_glu_ref[...])
    branch_b = vg[:, 0:D_MODEL] * jax.nn.sigmoid(vg[:, D_MODEL:2 * D_MODEL])
    g0 = POOL_WIDTH + SSM_WIDTH
    merged = (jax.nn.sigmoid(z_ref[:, g0:g0 + D_MODEL]) * ya_ref[...]
              + jax.nn.sigmoid(z_ref[:, g0 + D_MODEL:g0 + 2 * D_MODEL]) * branch_b).astype(BF16)
    if permute:
        merged = _dot(perm_t_ref[...], merged).astype(BF16)
    x1 = x + _dot(merged, w_out_ref[...])
    x1_ref[...] = x1.reshape(x1_ref.shape)


def _resident(shape):
    nd = len(shape)
    return pl.BlockSpec(shape, lambda i, _nd=nd: (0,) * _nd, pipeline_mode=pl.Buffered(1))


def _mixer_call(x, pool_in, h0, weights, *, n_seq, t_chunk, pos_start):
    rows = n_seq * t_chunk
    permute = t_chunk > 1
    has_state = pool_in is not None
    if permute:
        n_steps = x.shape[1] // t_chunk
        x_spec = pl.BlockSpec((n_seq, t_chunk, D_MODEL), lambda i: (0, i, 0))
    else:
        n_steps = 1
        x_spec = pl.BlockSpec((1, n_seq, D_MODEL), lambda i: (0, 0, 0))

    in_arrays = [x]
    in_specs = [x_spec]
    if has_state:
        in_arrays += [pool_in, h0]
        in_specs += [_resident(pool_in.shape), _resident(h0.shape)]
    if permute:
        r = np.arange(rows)
        perm = np.zeros((rows, rows), np.float32)
        perm[r, (r % n_seq) * t_chunk + r // n_seq] = 1.0
        in_arrays += [jnp.asarray(perm, BF16), jnp.asarray(perm.T, BF16)]
        in_specs += [_resident((rows, rows)), _resident((rows, rows))]
    in_arrays += list(weights)
    in_specs += [_resident(w.shape) for w in weights]

    out_shape = (
        jax.ShapeDtypeStruct(x.shape, F32),
        jax.ShapeDtypeStruct((POOL_SLOTS * n_seq, POOL_WIDTH), F32),
        jax.ShapeDtypeStruct((n_seq, STATE_LANES), F32),
    )
    out_specs = (
        x_spec,
        pl.BlockSpec((POOL_SLOTS * n_seq, POOL_WIDTH), lambda i: (0, 0)),
        pl.BlockSpec((n_seq, STATE_LANES), lambda i: (0, 0)),
    )
    scratch = [
        pltpu.VMEM((rows, IN_WIDTH), F32),
        pltpu.VMEM(((POOL_SLOTS + t_chunk) * n_seq, POOL_WIDTH), F32),
        pltpu.VMEM((rows, STATE_LANES), F32),
        pltpu.VMEM((n_seq, STATE_LANES), F32),
        pltpu.VMEM((rows, D_MODEL), F32),
    ]
    kern = functools.partial(_mixer_kernel, n_seq=n_seq, t_chunk=t_chunk, permute=permute,
                             pos_start=pos_start, has_state=has_state)
    return pl.pallas_call(
        kern,
        grid=(n_steps,),
        in_specs=in_specs,
        out_specs=out_specs,
        out_shape=out_shape,
        scratch_shapes=scratch,
        compiler_params=pltpu.CompilerParams(
            dimension_semantics=("arbitrary",), vmem_limit_bytes=VMEM_LIMIT_BYTES),
        name=f"mixer_n{n_seq}_t{t_chunk}",
    )(*in_arrays)


def _ffn_kernel(x1_ref, p_ref, g_ff_ref, w_ff1_ref, w_ff2_ref, g_ple_ref, w_ple_ref,
                w_ple_gate_ref, g_final_ref, out_ref):
    x1 = x1_ref[...]
    h2 = _rmsnorm(x1, g_ff_ref[...]).astype(BF16)
    x2 = x1
    for c in range(D_FF // FF_CHUNK):
        hid = _dot(h2, w_ff1_ref[:, c * FF_CHUNK:(c + 1) * FF_CHUNK])
        hid = jnp.square(jnp.maximum(hid, 0.0)).astype(BF16)
        x2 = x2 + _dot(hid, w_ff2_ref[c * FF_CHUNK:(c + 1) * FF_CHUNK, :])
    h3 = _rmsnorm(x2, g_ple_ref[...]).astype(BF16)
    gate = jax.nn.sigmoid(_dot(h3, w_ple_gate_ref[...]))
    x3 = x2 + _dot(p_ref[...].astype(BF16), w_ple_ref[...]) * gate
    out_ref[...] = _rmsnorm(x3, g_final_ref[...])


def _ffn_call(x1, p, weights, *, block_rows):
    n_rows = x1.shape[0]
    block_rows = min(block_rows, n_rows)
    in_specs = [pl.BlockSpec((block_rows, D_MODEL), lambda i: (i, 0)),
                pl.BlockSpec((block_rows, PLE_DIM), lambda i: (i, 0))]
    in_specs += [_resident(w.shape) for w in weights]
    return pl.pallas_call(
        _ffn_kernel,
        grid=(n_rows // block_rows,),
        in_specs=in_specs,
        out_specs=pl.BlockSpec((block_rows, D_MODEL), lambda i: (i, 0)),
        out_shape=jax.ShapeDtypeStruct((n_rows, D_MODEL), F32),
        compiler_params=pltpu.CompilerParams(
            dimension_semantics=("parallel",), vmem_limit_bytes=VMEM_LIMIT_BYTES),
        name=f"ffn_r{n_rows}",
    )(x1, p, *weights)


def _to_state_lanes(re, im):
    lead = re.shape[:-2]
    re = re.reshape(lead + (2, PART_LANES))
    im = im.reshape(lead + (2, PART_LANES))
    return jnp.stack([re, im], axis=-2).reshape(lead + (STATE_LANES,))


def _from_state_lanes(h):
    h = h.reshape(h.shape[0], 2, 2, PART_LANES)
    re = h[:, :, 0].reshape(h.shape[0], SSM_GROUPS, SSM_STATE)
    im = h[:, :, 1].reshape(h.shape[0], SSM_GROUPS, SSM_STATE)
    return re, im


def _s5_params(lam_re, lam_im, log_dt, b_re, b_im, c_re, c_im):
    dt = jnp.exp(log_dt)[:, None]
    mag = jnp.exp(lam_re * dt)
    ang = lam_im * dt
    abar_re = mag * jnp.cos(ang)
    abar_im = mag * jnp.sin(ang)
    den = lam_re * lam_re + lam_im * lam_im
    nr = abar_re - 1.0
    ni = abar_im
    k_re = ((nr * lam_re + ni * lam_im) / den)[:, :, None]
    k_im = ((ni * lam_re - nr * lam_im) / den)[:, :, None]
    bbar_re = k_re * b_re - k_im * b_im
    bbar_im = k_re * b_im + k_im * b_re
    abar = _to_state_lanes(abar_re, abar_im)[None, :]

    gh = SSM_GROUPS // 2
    eye = jnp.eye(gh, dtype=F32)

    def in_block(bb):
        return jnp.einsum('ab,aph->ahbp', eye, bb).reshape(gh * SSM_GROUP_DIM, PART_LANES)

    def out_block(cc):
        return jnp.einsum('ab,ahp->apbh', eye, cc).reshape(PART_LANES, gh * SSM_GROUP_DIM)

    bbar = jnp.stack([
        jnp.concatenate([in_block(bbar_re[hf * gh:(hf + 1) * gh]),
                         in_block(bbar_im[hf * gh:(hf + 1) * gh])], axis=1)
        for hf in range(2)]).astype(BF16)
    cmat = jnp.stack([
        jnp.concatenate([out_block(c_re[hf * gh:(hf + 1) * gh]),
                         out_block(-c_im[hf * gh:(hf + 1) * gh])], axis=0)
        for hf in range(2)]).astype(BF16)
    return abar, bbar, cmat


def kernel(x_prompt, x_sample, p_prompt, p_sample, state_pool, state_ssm_re, state_ssm_im, g_mix, w_in, w_pool, pool_scale, lam_re, lam_im, log_dt, b_re, b_im, c_re, c_im, d_skip, w_glu_v, w_glu_g, w_out, g_ff, w_ff1, w_ff2, g_ple, w_ple, w_ple_gate, g_final):
    assert w_in.shape[0] == 1, "the final norm is fused into the (single) layer's ffn call"
    batch, seq_len, _ = x_prompt.shape
    dec_batch = x_sample.shape[0]
    row = lambda v: v.reshape(1, -1).astype(F32)

    abar, bbar, cmat = _s5_params(lam_re[0], lam_im[0], log_dt[0], b_re[0], b_im[0],
                                  c_re[0], c_im[0])
    mixer_w = (row(g_mix[0]), w_in[0].astype(BF16), w_pool[0].astype(BF16),
               row(pool_scale[0]), abar, bbar, cmat, row(d_skip[0]),
               jnp.concatenate([w_glu_v[0], w_glu_g[0]], axis=1).astype(BF16),
               w_out[0].astype(BF16))
    ffn_w = (row(g_ff[0]), w_ff1[0].astype(BF16), w_ff2[0].astype(BF16), row(g_ple[0]),
             w_ple[0].astype(BF16), w_ple_gate[0].astype(BF16), row(g_final))

    x1p, pool_p, h_p = _mixer_call(x_prompt, None, None, mixer_w, n_seq=batch,
                                   t_chunk=PROMPT_T_CHUNK, pos_start=0)
    y_prompt = _ffn_call(x1p.reshape(batch * seq_len, D_MODEL),
                         p_prompt[0].reshape(batch * seq_len, PLE_DIM), ffn_w,
                         block_rows=FFN_BLOCK_ROWS).reshape(batch, seq_len, D_MODEL)
    pool_tm = jnp.concatenate(
        [jnp.zeros((1, dec_batch, POOL_WIDTH), F32), jnp.swapaxes(state_pool[0], 0, 1)],
        axis=0).reshape(POOL_SLOTS * dec_batch, POOL_WIDTH)
    h0 = _to_state_lanes(state_ssm_re[0], state_ssm_im[0])
    x1s, pool_s, h_s = _mixer_call(x_sample.reshape(1, dec_batch, D_MODEL), pool_tm, h0, mixer_w,
                                   n_seq=dec_batch, t_chunk=1, pos_start=PAST_LEN)
    y_sample = _ffn_call(x1s.reshape(dec_batch, D_MODEL), p_sample[0].reshape(dec_batch, PLE_DIM),
                         ffn_w, block_rows=dec_batch).reshape(dec_batch, 1, D_MODEL)

    def to_buf(tm, n):
        return jnp.swapaxes(tm.reshape(POOL_SLOTS, n, POOL_WIDTH)[1:], 0, 1)[None]

    re_p, im_p = _from_state_lanes(h_p)
    re_s, im_s = _from_state_lanes(h_s)
    return (y_prompt, y_sample, to_buf(pool_p, batch), to_buf(pool_s, dec_batch),
            re_p[None], im_p[None], re_s[None], im_s[None])
```

```python
import functools

import jax
import jax.numpy as jnp
import numpy as np
from jax import lax
from jax.experimental import pallas as pl
from jax.experimental.pallas import tpu as pltpu

D_MODEL = 1024
POOL_WINDOWS = (2, 4, 8, 16)
POOL_WIDTH = D_MODEL // 2
POOL_GROUP_DIM = POOL_WIDTH // len(POOL_WINDOWS)
POOL_OUT_GROUP_DIM = D_MODEL // len(POOL_WINDOWS)
POOL_BUF = max(POOL_WINDOWS) - 1
POOL_SLOTS = POOL_BUF + 1
SSM_WIDTH = D_MODEL // 2
SSM_GROUP_DIM = 16
SSM_GROUPS = SSM_WIDTH // SSM_GROUP_DIM
SSM_STATE = 64
PART_LANES = SSM_GROUPS * SSM_STATE
STATE_LANES = 2 * PART_LANES
SSM_HALVES = 2
HALF_IN = SSM_WIDTH // SSM_HALVES
HALF_GROUPS = SSM_GROUPS // SSM_HALVES
HALF_LANES = PART_LANES // SSM_HALVES
D_FF = 4 * D_MODEL
FF_CHUNK = 1024
PLE_DIM = 256
EPS = 1e-6
IN_WIDTH = POOL_WIDTH + SSM_WIDTH + 2 * D_MODEL
PAST_LEN = 16384
PROMPT_T_CHUNK = 64
FFN_BLOCK_ROWS = 512
SCAN_LANES = 512
VMEM_LIMIT_BYTES = 58 * 1024 * 1024

BF16 = jnp.bfloat16
F32 = jnp.float32


def _rmsnorm(x, g):
    return x * lax.rsqrt(jnp.mean(x * x, axis=-1, keepdims=True) + EPS) * g


def _dot(a, b):
    return jnp.dot(a, b, preferred_element_type=F32)


def _mixer_kernel(*refs, n_seq, t_chunk, permute, pos_start, has_state):
    it = iter(refs)
    x_ref = next(it)
    pool_in_ref = next(it) if has_state else None
    h0_ref = next(it) if has_state else None
    perm_ref = next(it) if permute else None
    perm_t_ref = next(it) if permute else None
    vec_ref, w_in_ref, w_pool_ref, abar_ref, bbar_ref, cmat_ref, w_glu_ref, w_out_ref = (
        next(it) for _ in range(8))
    x1_ref, ua_out_ref, h_out_ref = (next(it) for _ in range(3))
    z_ref, ext_ref, hs_ref, hc_ref, ya_ref = (next(it) for _ in range(5))

    rows = n_seq * t_chunk
    step = pl.program_id(0)
    cur0 = POOL_SLOTS * n_seq
    g_mix = vec_ref[0:1, :]
    pool_scale = vec_ref[1:2, :]
    d_skip = vec_ref[2:3, 0:SSM_WIDTH]

    @pl.when(step == 0)
    def _init():
        if has_state:
            ext_ref[n_seq:cur0, :] = pool_in_ref[...]
            hc_ref[...] = h0_ref[...]
        else:
            ext_ref[0:cur0, :] = jnp.zeros((cur0, POOL_WIDTH), F32)
            hc_ref[...] = jnp.zeros((n_seq, STATE_LANES), F32)

    x = x_ref[...].reshape(rows, D_MODEL)
    h = _rmsnorm(x, g_mix).astype(BF16)
    if permute:
        h = _dot(perm_ref[...], h).astype(BF16)
    z_ref[...] = _dot(h, w_in_ref[...])

    ext_ref[cur0:cur0 + rows, :] = z_ref[:, 0:POOL_WIDTH]
    t_idx = lax.shift_right_logical(
        lax.broadcasted_iota(jnp.int32, (rows, POOL_GROUP_DIM), 0), n_seq.bit_length() - 1)
    pos = t_idx + (pos_start + step * t_chunk)
    for gi, w in enumerate(POOL_WINDOWS):
        c0 = gi * POOL_GROUP_DIM
        cur = ext_ref[cur0:cur0 + rows, c0:c0 + POOL_GROUP_DIM]
        acc = cur
        for k in range(1, w):
            r0 = cur0 - k * n_seq
            acc = acc + ext_ref[r0:r0 + rows, c0:c0 + POOL_GROUP_DIM]
        count = jnp.minimum(pos + 1, w).astype(F32)
        pooled = (acc / count - cur).astype(BF16)
        o0 = gi * POOL_OUT_GROUP_DIM
        ya_ref[:, o0:o0 + POOL_OUT_GROUP_DIM] = (
            _dot(pooled, w_pool_ref[gi]) * pool_scale[:, o0:o0 + POOL_OUT_GROUP_DIM])
    keep = ua_out_ref.shape[0]
    ua_out_ref[...] = ext_ref[cur0 + rows - keep:cur0 + rows, :]
    if t_chunk >= POOL_SLOTS:
        ext_ref[0:cur0, :] = ext_ref[rows:rows + cur0, :]

    u_b = z_ref[:, POOL_WIDTH:POOL_WIDTH + SSM_WIDTH]
    u_b16 = u_b.astype(BF16)
    for hf in range(SSM_HALVES):
        bu = _dot(u_b16[:, hf * HALF_IN:(hf + 1) * HALF_IN], bbar_ref[hf])
        hs_ref[:, hf * HALF_LANES:(hf + 1) * HALF_LANES] = bu[:, 0:HALF_LANES]
        hs_ref[:, PART_LANES + hf * HALF_LANES:PART_LANES + (hf + 1) * HALF_LANES] = (
            bu[:, HALF_LANES:2 * HALF_LANES])

    for q in range(PART_LANES // SCAN_LANES):
        re0 = q * SCAN_LANES
        im0 = PART_LANES + re0
        a_re = abar_ref[:, re0:re0 + SCAN_LANES]
        a_im = abar_ref[:, im0:im0 + SCAN_LANES]
        if t_chunk > 1:
            a_re = jnp.broadcast_to(a_re, (n_seq, SCAN_LANES))
            a_im = jnp.broadcast_to(a_im, (n_seq, SCAN_LANES))

        def scan_step(t, carry, re0=re0, im0=im0, a_re=a_re, a_im=a_im):
            h_re, h_im = carry
            r0 = t * n_seq if isinstance(t, int) else pl.multiple_of(t * n_seq, n_seq)
            n_re = a_re * h_re - a_im * h_im + hs_ref[pl.ds(r0, n_seq), re0:re0 + SCAN_LANES]
            n_im = a_re * h_im + a_im * h_re + hs_ref[pl.ds(r0, n_seq), im0:im0 + SCAN_LANES]
            hs_ref[pl.ds(r0, n_seq), re0:re0 + SCAN_LANES] = n_re
            hs_ref[pl.ds(r0, n_seq), im0:im0 + SCAN_LANES] = n_im
            return n_re, n_im

        carry = (hc_ref[:, re0:re0 + SCAN_LANES], hc_ref[:, im0:im0 + SCAN_LANES])
        if t_chunk == 1:
            carry = scan_step(0, carry)
        else:
            carry = lax.fori_loop(0, t_chunk, scan_step, carry, unroll=8)
        hc_ref[:, re0:re0 + SCAN_LANES] = carry[0]
        hc_ref[:, im0:im0 + SCAN_LANES] = carry[1]
    h_out_ref[...] = hc_ref[...]

    y_halves = []
    for hf in range(SSM_HALVES):
        l0 = hf * HALF_LANES
        y_halves.append(
            _dot(hs_ref[:, l0:l0 + HALF_LANES].astype(BF16), cmat_ref[0, hf])
            + _dot(hs_ref[:, PART_LANES + l0:PART_LANES + l0 + HALF_LANES].astype(BF16),
                   cmat_ref[1, hf]))
    s = jnp.concatenate(y_halves, axis=-1) + d_skip * u_b

    gl = jax.nn.gelu(s).astype(BF16)
    vg = _dot(gl, w_glu_ref[...])
    branch_b = vg[:, 0:D_MODEL] * jax.nn.sigmoid(vg[:, D_MODEL:2 * D_MODEL])
    g0 = POOL_WIDTH + SSM_WIDTH
    merged = (jax.nn.sigmoid(z_ref[:, g0:g0 + D_MODEL]) * ya_ref[...]
              + jax.nn.sigmoid(z_ref[:, g0 + D_MODEL:g0 + 2 * D_MODEL]) * branch_b).astype(BF16)
    if permute:
        merged = _dot(perm_t_ref[...], merged).astype(BF16)
    x1 = x + _dot(merged, w_out_ref[...])
    x1_ref[...] = x1.reshape(x1_ref.shape)


def _resident(shape):
    nd = len(shape)
    return pl.BlockSpec(shape, lambda i, _nd=nd: (0,) * _nd, pipeline_mode=pl.Buffered(1))


def _mixer_call(x, pool_in, h0, weights, *, n_seq, t_chunk, pos_start):
    rows = n_seq * t_chunk
    permute = t_chunk > 1
    has_state = pool_in is not None
    if permute:
        n_steps = x.shape[1] // t_chunk
        x_spec = pl.BlockSpec((n_seq, t_chunk, D_MODEL), lambda i: (0, i, 0))
    else:
        n_steps = 1
        x_spec = pl.BlockSpec((1, n_seq, D_MODEL), lambda i: (0, 0, 0))
    keep_rows = min(POOL_BUF, t_chunk) * n_seq

    in_arrays = [x]
    in_specs = [x_spec]
    if has_state:
        in_arrays += [pool_in, h0]
        in_specs += [_resident(pool_in.shape), _resident(h0.shape)]
    if permute:
        r = np.arange(rows)
        perm = np.zeros((rows, rows), np.float32)
        perm[r, (r % n_seq) * t_chunk + r // n_seq] = 1.0
        in_arrays += [jnp.asarray(perm, BF16), jnp.asarray(perm.T, BF16)]
        in_specs += [_resident((rows, rows)), _resident((rows, rows))]
    in_arrays += list(weights)
    in_specs += [_resident(w.shape) for w in weights]

    out_shape = (
        jax.ShapeDtypeStruct(x.shape, F32),
        jax.ShapeDtypeStruct((keep_rows, POOL_WIDTH), F32),
        jax.ShapeDtypeStruct((n_seq, STATE_LANES), F32),
    )
    out_specs = (
        x_spec,
        pl.BlockSpec((keep_rows, POOL_WIDTH), lambda i: (0, 0)),
        pl.BlockSpec((n_seq, STATE_LANES), lambda i: (0, 0)),
    )
    scratch = [
        pltpu.VMEM((rows, IN_WIDTH), F32),
        pltpu.VMEM(((POOL_SLOTS + t_chunk) * n_seq, POOL_WIDTH), F32),
        pltpu.VMEM((rows, STATE_LANES), F32),
        pltpu.VMEM((n_seq, STATE_LANES), F32),
        pltpu.VMEM((rows, D_MODEL), F32),
    ]
    kern = functools.partial(_mixer_kernel, n_seq=n_seq, t_chunk=t_chunk, permute=permute,
                             pos_start=pos_start, has_state=has_state)
    return pl.pallas_call(
        kern,
        grid=(n_steps,),
        in_specs=in_specs,
        out_specs=out_specs,
        out_shape=out_shape,
        scratch_shapes=scratch,
        compiler_params=pltpu.CompilerParams(
            dimension_semantics=("arbitrary",), vmem_limit_bytes=VMEM_LIMIT_BYTES),
        name=f"mixer_n{n_seq}_t{t_chunk}",
    )(*in_arrays)


def _ffn_kernel(x1_ref, p_ref, vec_ref, w_ff1_ref, w_ff2_ref, w_ple_ref, w_ple_gate_ref, out_ref):
    x1 = x1_ref[...]
    h2 = _rmsnorm(x1, vec_ref[0:1, :]).astype(BF16)
    x2 = x1
    for c in range(D_FF // FF_CHUNK):
        hid = _dot(h2, w_ff1_ref[:, c * FF_CHUNK:(c + 1) * FF_CHUNK])
        hid = jnp.square(jnp.maximum(hid, 0.0)).astype(BF16)
        x2 = x2 + _dot(hid, w_ff2_ref[c * FF_CHUNK:(c + 1) * FF_CHUNK, :])
    h3 = _rmsnorm(x2, vec_ref[1:2, :]).astype(BF16)
    gate = jax.nn.sigmoid(_dot(h3, w_ple_gate_ref[...]))
    x3 = x2 + _dot(p_ref[...].astype(BF16), w_ple_ref[...]) * gate
    out_ref[...] = _rmsnorm(x3, vec_ref[2:3, :])


def _ffn_call(x1, p, weights, *, block_rows):
    n_rows = x1.shape[0]
    block_rows = min(block_rows, n_rows)
    in_specs = [pl.BlockSpec((block_rows, D_MODEL), lambda i: (i, 0)),
                pl.BlockSpec((block_rows, PLE_DIM), lambda i: (i, 0))]
    in_specs += [_resident(w.shape) for w in weights]
    return pl.pallas_call(
        _ffn_kernel,
        grid=(n_rows // block_rows,),
        in_specs=in_specs,
        out_specs=pl.BlockSpec((block_rows, D_MODEL), lambda i: (i, 0)),
        out_shape=jax.ShapeDtypeStruct((n_rows, D_MODEL), F32),
        compiler_params=pltpu.CompilerParams(
            dimension_semantics=("parallel",), vmem_limit_bytes=VMEM_LIMIT_BYTES),
        name=f"ffn_r{n_rows}",
    )(x1, p, *weights)


def _s5_params(lam_re, lam_im, log_dt, b_re, b_im, c_re, c_im):
    dt = jnp.exp(log_dt)[:, None]
    mag = jnp.exp(lam_re * dt)
    ang = lam_im * dt
    abar_re = mag * jnp.cos(ang)
    abar_im = mag * jnp.sin(ang)
    den = lam_re * lam_re + lam_im * lam_im
    nr = abar_re - 1.0
    ni = abar_im
    k_re = ((nr * lam_re + ni * lam_im) / den)[:, :, None]
    k_im = ((ni * lam_re - nr * lam_im) / den)[:, :, None]
    bbar_re = k_re * b_re - k_im * b_im
    bbar_im = k_re * b_im + k_im * b_re
    abar = jnp.concatenate([abar_re.reshape(1, PART_LANES), abar_im.reshape(1, PART_LANES)],
                           axis=1)

    same_group = jnp.eye(HALF_GROUPS, dtype=F32)
    bb = jnp.stack([bbar_re, bbar_im]).reshape(
        2, SSM_HALVES, HALF_GROUPS, SSM_STATE, SSM_GROUP_DIM)
    bb = jnp.transpose(bb, (1, 2, 4, 0, 3))
    bbar = (bb[:, :, :, :, None, :] * same_group[None, :, None, None, :, None]).reshape(
        SSM_HALVES, HALF_IN, 2 * HALF_LANES).astype(BF16)
    cc = jnp.stack([c_re, -c_im]).reshape(
        2, SSM_HALVES, HALF_GROUPS, SSM_GROUP_DIM, SSM_STATE)
    cc = jnp.transpose(cc, (0, 1, 2, 4, 3))
    cmat = (cc[:, :, :, :, None, :] * same_group[None, None, :, None, :, None]).reshape(
        2, SSM_HALVES, HALF_LANES, HALF_IN).astype(BF16)
    return abar, bbar, cmat


def kernel(x_prompt, x_sample, p_prompt, p_sample, state_pool, state_ssm_re, state_ssm_im, g_mix, w_in, w_pool, pool_scale, lam_re, lam_im, log_dt, b_re, b_im, c_re, c_im, d_skip, w_glu_v, w_glu_g, w_out, g_ff, w_ff1, w_ff2, g_ple, w_ple, w_ple_gate, g_final):
    assert w_in.shape[0] == 1, "the final norm is fused into the (single) layer's ffn call"
    batch, seq_len, _ = x_prompt.shape
    dec_batch = x_sample.shape[0]

    abar, bbar, cmat = _s5_params(lam_re[0], lam_im[0], log_dt[0], b_re[0], b_im[0],
                                  c_re[0], c_im[0])
    mixer_vec = jnp.stack([g_mix[0], pool_scale[0],
                           jnp.pad(d_skip[0], (0, D_MODEL - SSM_WIDTH))])
    mixer_w = (mixer_vec, w_in[0].astype(BF16), w_pool[0].astype(BF16), abar, bbar, cmat,
               jnp.concatenate([w_glu_v[0], w_glu_g[0]], axis=1).astype(BF16),
               w_out[0].astype(BF16))
    ffn_w = (jnp.stack([g_ff[0], g_ple[0], g_final]), w_ff1[0].astype(BF16),
             w_ff2[0].astype(BF16), w_ple[0].astype(BF16), w_ple_gate[0].astype(BF16))

    x1p, pool_p, h_p = _mixer_call(x_prompt, None, None, mixer_w, n_seq=batch,
                                   t_chunk=PROMPT_T_CHUNK, pos_start=0)
    y_prompt = _ffn_call(x1p.reshape(batch * seq_len, D_MODEL),
                         p_prompt.reshape(batch * seq_len, PLE_DIM), ffn_w,
                         block_rows=FFN_BLOCK_ROWS).reshape(batch, seq_len, D_MODEL)
    pool_tm = jnp.swapaxes(state_pool[0], 0, 1).reshape(POOL_BUF * dec_batch, POOL_WIDTH)
    h0 = jnp.concatenate([state_ssm_re.reshape(dec_batch, PART_LANES),
                          state_ssm_im.reshape(dec_batch, PART_LANES)], axis=1)
    x1s, ua_s, h_s = _mixer_call(x_sample.reshape(1, dec_batch, D_MODEL), pool_tm, h0, mixer_w,
                                 n_seq=dec_batch, t_chunk=1, pos_start=PAST_LEN)
    y_sample = _ffn_call(x1s.reshape(dec_batch, D_MODEL), p_sample.reshape(dec_batch, PLE_DIM),
                         ffn_w, block_rows=dec_batch).reshape(dec_batch, 1, D_MODEL)

    new_pool_p = jnp.swapaxes(pool_p.reshape(POOL_BUF, batch, POOL_WIDTH), 0, 1)[None]
    new_pool_s = jnp.concatenate([state_pool[:, :, 1:], ua_s[None, :, None, :]], axis=2)
    state_shape = lambda n: (1, n, SSM_GROUPS, SSM_STATE)
    return (y_prompt, y_sample, new_pool_p, new_pool_s,
            h_p[:, :PART_LANES].reshape(state_shape(batch)),
            h_p[:, PART_LANES:].reshape(state_shape(batch)),
            h_s[:, :PART_LANES].reshape(state_shape(dec_batch)),
            h_s[:, PART_LANES:].reshape(state_shape(dec_batch)))
```

```python
import functools

import jax
import jax.numpy as jnp
import numpy as np
from jax import lax
from jax.experimental import pallas as pl
from jax.experimental.pallas import tpu as pltpu

D_MODEL = 1024
POOL_WINDOWS = (2, 4, 8, 16)
POOL_WIDTH = D_MODEL // 2
POOL_GROUP_DIM = POOL_WIDTH // len(POOL_WINDOWS)
POOL_OUT_GROUP_DIM = D_MODEL // len(POOL_WINDOWS)
POOL_BUF = max(POOL_WINDOWS) - 1
POOL_SLOTS = POOL_BUF + 1
SSM_WIDTH = D_MODEL // 2
SSM_GROUP_DIM = 16
SSM_GROUPS = SSM_WIDTH // SSM_GROUP_DIM
SSM_STATE = 64
PART_LANES = SSM_GROUPS * SSM_STATE
STATE_LANES = 2 * PART_LANES
SSM_HALVES = 2
HALF_IN = SSM_WIDTH // SSM_HALVES
HALF_GROUPS = SSM_GROUPS // SSM_HALVES
HALF_LANES = PART_LANES // SSM_HALVES
D_FF = 4 * D_MODEL
FF_CHUNK = 1024
PLE_DIM = 256
EPS = 1e-6
IN_WIDTH = POOL_WIDTH + SSM_WIDTH + 2 * D_MODEL
PAST_LEN = 16384
PROMPT_T_CHUNK = 64
PERM_T = 32
FFN_BLOCK_ROWS = 512
SCAN_LANES = 512
SCAN_UNROLL = 4
BF16_TILE_ROWS = 16
VMEM_LIMIT_BYTES = 58 * 1024 * 1024

BF16 = jnp.bfloat16
F32 = jnp.float32


def _rmsnorm(x, g):
    return x * lax.rsqrt(jnp.mean(x * x, axis=-1, keepdims=True) + EPS) * g


def _dot(a, b):
    return jnp.dot(a, b, preferred_element_type=F32)


def _mixer_kernel(*refs, n_seq, t_chunk, permute, pos_start, has_state):
    it = iter(refs)
    x_ref = next(it)
    pool_in_ref = next(it) if has_state else None
    h0_ref = next(it) if has_state else None
    perm_ref = next(it) if permute else None
    perm_t_ref = next(it) if permute else None
    vec_ref, w_in_ref, w_pool_ref, abar_ref, bbar_ref, cmat_ref, w_glu_ref, w_out_ref = (
        next(it) for _ in range(8))
    x1_ref, ua_out_ref, h_out_ref = (next(it) for _ in range(3))
    z_ref, ext_ref, hs_ref, h16_ref, hc_ref, ya_ref = (next(it) for _ in range(6))

    rows = n_seq * t_chunk
    step = pl.program_id(0)
    cur0 = POOL_SLOTS * n_seq
    g_mix = vec_ref[0:1, :]
    pool_scale = vec_ref[1:2, :]
    d_skip = vec_ref[2:3, 0:SSM_WIDTH]

    @pl.when(step == 0)
    def _init():
        if has_state:
            ext_ref[n_seq:cur0, :] = pool_in_ref[...]
            hc_ref[...] = h0_ref[...]
        else:
            ext_ref[0:cur0, :] = jnp.zeros((cur0, POOL_WIDTH), F32)
            hc_ref[...] = jnp.zeros((n_seq, STATE_LANES), F32)

    x = x_ref[...].reshape(rows, D_MODEL)
    h = _rmsnorm(x, g_mix).astype(BF16)
    if permute:
        blocks = []
        for j in range(t_chunk // PERM_T):
            picked = jnp.concatenate(
                [h[b * t_chunk + j * PERM_T:b * t_chunk + (j + 1) * PERM_T] for b in range(n_seq)],
                axis=0)
            blocks.append(_dot(perm_ref[...], picked).astype(BF16))
        h = jnp.concatenate(blocks, axis=0)
    z_ref[...] = _dot(h, w_in_ref[...])

    ext_ref[cur0:cur0 + rows, :] = z_ref[:, 0:POOL_WIDTH]
    t_idx = lax.shift_right_logical(
        lax.broadcasted_iota(jnp.int32, (rows, POOL_GROUP_DIM), 0), n_seq.bit_length() - 1)
    pos = t_idx + (pos_start + step * t_chunk)
    for gi, w in enumerate(POOL_WINDOWS):
        c0 = gi * POOL_GROUP_DIM
        cur = ext_ref[cur0:cur0 + rows, c0:c0 + POOL_GROUP_DIM]
        acc = cur
        for k in range(1, w):
            r0 = cur0 - k * n_seq
            acc = acc + ext_ref[r0:r0 + rows, c0:c0 + POOL_GROUP_DIM]
        count = jnp.minimum(pos + 1, w).astype(F32)
        pooled = (acc / count - cur).astype(BF16)
        o0 = gi * POOL_OUT_GROUP_DIM
        ya_ref[:, o0:o0 + POOL_OUT_GROUP_DIM] = (
            _dot(pooled, w_pool_ref[gi]) * pool_scale[:, o0:o0 + POOL_OUT_GROUP_DIM])
    keep = ua_out_ref.shape[0]
    ua_out_ref[...] = ext_ref[cur0 + rows - keep:cur0 + rows, :]
    if t_chunk >= POOL_SLOTS:
        ext_ref[0:cur0, :] = ext_ref[rows:rows + cur0, :]

    u_b = z_ref[:, POOL_WIDTH:POOL_WIDTH + SSM_WIDTH]
    u_b16 = u_b.astype(BF16)
    for hf in range(SSM_HALVES):
        bu = _dot(u_b16[:, hf * HALF_IN:(hf + 1) * HALF_IN], bbar_ref[hf])
        hs_ref[:, hf * HALF_LANES:(hf + 1) * HALF_LANES] = bu[:, 0:HALF_LANES]
        hs_ref[:, PART_LANES + hf * HALF_LANES:PART_LANES + (hf + 1) * HALF_LANES] = (
            bu[:, HALF_LANES:2 * HALF_LANES])

    steps_per_store = max(1, BF16_TILE_ROWS // n_seq)
    store_rows = steps_per_store * n_seq
    for q in range(PART_LANES // SCAN_LANES):
        re0 = q * SCAN_LANES
        im0 = PART_LANES + re0
        a_re = abar_ref[:, re0:re0 + SCAN_LANES]
        a_im = abar_ref[:, im0:im0 + SCAN_LANES]
        if t_chunk > 1:
            a_re = jnp.broadcast_to(a_re, (n_seq, SCAN_LANES))
            a_im = jnp.broadcast_to(a_im, (n_seq, SCAN_LANES))

        def scan_steps(i, carry, re0=re0, im0=im0, a_re=a_re, a_im=a_im):
            h_re, h_im = carry
            r0 = i * store_rows if isinstance(i, int) else pl.multiple_of(i * store_rows,
                                                                          store_rows)
            new_re, new_im = [], []
            for s in range(steps_per_store):
                rs = r0 + s * n_seq
                h_re, h_im = (
                    a_re * h_re - a_im * h_im + hs_ref[pl.ds(rs, n_seq), re0:re0 + SCAN_LANES],
                    a_re * h_im + a_im * h_re + hs_ref[pl.ds(rs, n_seq), im0:im0 + SCAN_LANES])
                new_re.append(h_re)
                new_im.append(h_im)
            h16_ref[pl.ds(r0, store_rows), re0:re0 + SCAN_LANES] = (
                jnp.concatenate(new_re, axis=0).astype(BF16))
            h16_ref[pl.ds(r0, store_rows), im0:im0 + SCAN_LANES] = (
                jnp.concatenate(new_im, axis=0).astype(BF16))
            return h_re, h_im

        carry = (hc_ref[:, re0:re0 + SCAN_LANES], hc_ref[:, im0:im0 + SCAN_LANES])
        if t_chunk == steps_per_store:
            carry = scan_steps(0, carry)
        else:
            carry = lax.fori_loop(0, t_chunk // steps_per_store, scan_steps, carry,
                                  unroll=SCAN_UNROLL)
        hc_ref[:, re0:re0 + SCAN_LANES] = carry[0]
        hc_ref[:, im0:im0 + SCAN_LANES] = carry[1]
    h_out_ref[...] = hc_ref[...]

    y_halves = []
    for hf in range(SSM_HALVES):
        l0 = hf * HALF_LANES
        y_halves.append(
            _dot(h16_ref[:, l0:l0 + HALF_LANES], cmat_ref[0, hf])
            + _dot(h16_ref[:, PART_LANES + l0:PART_LANES + l0 + HALF_LANES], cmat_ref[1, hf]))
    s = jnp.concatenate(y_halves, axis=-1) + d_skip * u_b

    gl = jax.nn.gelu(s).astype(BF16)
    vg = _dot(gl, w_glu_ref[...])
    branch_b = vg[:, 0:D_MODEL] * jax.nn.sigmoid(vg[:, D_MODEL:2 * D_MODEL])
    g0 = POOL_WIDTH + SSM_WIDTH
    merged = (jax.nn.sigmoid(z_ref[:, g0:g0 + D_MODEL]) * ya_ref[...]
              + jax.nn.sigmoid(z_ref[:, g0 + D_MODEL:g0 + 2 * D_MODEL]) * branch_b).astype(BF16)
    if permute:
        block_rows = n_seq * PERM_T
        blocks = [_dot(perm_t_ref[...], merged[j * block_rows:(j + 1) * block_rows]).astype(BF16)
                  for j in range(t_chunk // PERM_T)]
        merged = jnp.concatenate(
            [blk[b * PERM_T:(b + 1) * PERM_T] for b in range(n_seq) for blk in blocks], axis=0)
    x1 = x + _dot(merged, w_out_ref[...])
    x1_ref[...] = x1.reshape(x1_ref.shape)


def _resident(shape):
    nd = len(shape)
    return pl.BlockSpec(shape, lambda i, _nd=nd: (0,) * _nd, pipeline_mode=pl.Buffered(1))


def _mixer_call(x, pool_in, h0, weights, *, n_seq, t_chunk, pos_start):
    rows = n_seq * t_chunk
    permute = t_chunk > 1
    has_state = pool_in is not None
    if permute:
        n_steps = x.shape[1] // t_chunk
        x_spec = pl.BlockSpec((n_seq, t_chunk, D_MODEL), lambda i: (0, i, 0))
    else:
        n_steps = 1
        x_spec = pl.BlockSpec((1, n_seq, D_MODEL), lambda i: (0, 0, 0))
    keep_rows = min(POOL_BUF, t_chunk) * n_seq

    in_arrays = [x]
    in_specs = [x_spec]
    if has_state:
        in_arrays += [pool_in, h0]
        in_specs += [_resident(pool_in.shape), _resident(h0.shape)]
    if permute:
        block_rows = n_seq * PERM_T
        r = np.arange(block_rows)
        perm = np.zeros((block_rows, block_rows), np.float32)
        perm[r, (r % n_seq) * PERM_T + r // n_seq] = 1.0
        in_arrays += [jnp.asarray(perm, BF16), jnp.asarray(perm.T, BF16)]
        in_specs += [_resident(perm.shape), _resident(perm.shape)]
    in_arrays += list(weights)
    in_specs += [_resident(w.shape) for w in weights]

    out_shape = (
        jax.ShapeDtypeStruct(x.shape, F32),
        jax.ShapeDtypeStruct((keep_rows, POOL_WIDTH), F32),
        jax.ShapeDtypeStruct((n_seq, STATE_LANES), F32),
    )
    out_specs = (
        x_spec,
        pl.BlockSpec((keep_rows, POOL_WIDTH), lambda i: (0, 0)),
        pl.BlockSpec((n_seq, STATE_LANES), lambda i: (0, 0)),
    )
    scratch = [
        pltpu.VMEM((rows, IN_WIDTH), F32),
        pltpu.VMEM(((POOL_SLOTS + t_chunk) * n_seq, POOL_WIDTH), F32),
        pltpu.VMEM((rows, STATE_LANES), F32),
        pltpu.VMEM((rows, STATE_LANES), BF16),
        pltpu.VMEM((n_seq, STATE_LANES), F32),
        pltpu.VMEM((rows, D_MODEL), F32),
    ]
    kern = functools.partial(_mixer_kernel, n_seq=n_seq, t_chunk=t_chunk, permute=permute,
                             pos_start=pos_start, has_state=has_state)
    return pl.pallas_call(
        kern,
        grid=(n_steps,),
        in_specs=in_specs,
        out_specs=out_specs,
        out_shape=out_shape,
        scratch_shapes=scratch,
        compiler_params=pltpu.CompilerParams(
            dimension_semantics=("arbitrary",), vmem_limit_bytes=VMEM_LIMIT_BYTES),
        name=f"mixer_n{n_seq}_t{t_chunk}",
    )(*in_arrays)


def _ffn_kernel(x1_ref, p_ref, vec_ref, w_ff1_ref, w_ff2_ref, w_ple_ref, w_ple_gate_ref, out_ref):
    x1 = x1_ref[...]
    h2 = _rmsnorm(x1, vec_ref[0:1, :]).astype(BF16)
    x2 = x1
    for c in range(D_FF // FF_CHUNK):
        hid = _dot(h2, w_ff1_ref[:, c * FF_CHUNK:(c + 1) * FF_CHUNK])
        hid = jnp.square(jnp.maximum(hid, 0.0)).astype(BF16)
        x2 = x2 + _dot(hid, w_ff2_ref[c * FF_CHUNK:(c + 1) * FF_CHUNK, :])
    h3 = _rmsnorm(x2, vec_ref[1:2, :]).astype(BF16)
    gate = jax.nn.sigmoid(_dot(h3, w_ple_gate_ref[...]))
    x3 = x2 + _dot(p_ref[...].astype(BF16), w_ple_ref[...]) * gate
    out_ref[...] = _rmsnorm(x3, vec_ref[2:3, :])


def _ffn_call(x1, p, weights, *, block_rows):
    n_rows = x1.shape[0]
    block_rows = min(block_rows, n_rows)
    in_specs = [pl.BlockSpec((block_rows, D_MODEL), lambda i: (i, 0)),
                pl.BlockSpec((block_rows, PLE_DIM), lambda i: (i, 0))]
    in_specs += [_resident(w.shape) for w in weights]
    return pl.pallas_call(
        _ffn_kernel,
        grid=(n_rows // block_rows,),
        in_specs=in_specs,
        out_specs=pl.BlockSpec((block_rows, D_MODEL), lambda i: (i, 0)),
        out_shape=jax.ShapeDtypeStruct((n_rows, D_MODEL), F32),
        compiler_params=pltpu.CompilerParams(
            dimension_semantics=("parallel",), vmem_limit_bytes=VMEM_LIMIT_BYTES),
        name=f"ffn_r{n_rows}",
    )(x1, p, *weights)


def _s5_params(lam_re, lam_im, log_dt, b_re, b_im, c_re, c_im):
    dt = jnp.exp(log_dt)[:, None]
    mag = jnp.exp(lam_re * dt)
    ang = lam_im * dt
    abar_re = mag * jnp.cos(ang)
    abar_im = mag * jnp.sin(ang)
    den = lam_re * lam_re + lam_im * lam_im
    nr = abar_re - 1.0
    ni = abar_im
    k_re = ((nr * lam_re + ni * lam_im) / den)[:, :, None]
    k_im = ((ni * lam_re - nr * lam_im) / den)[:, :, None]
    bbar_re = k_re * b_re - k_im * b_im
    bbar_im = k_re * b_im + k_im * b_re
    abar = jnp.concatenate([abar_re.reshape(1, PART_LANES), abar_im.reshape(1, PART_LANES)],
                           axis=1)

    bb = jnp.transpose(jnp.stack([bbar_re, bbar_im]), (1, 3, 0, 2))
    bbar = _tile_block_diag(bb.reshape(2 * SSM_WIDTH, SSM_STATE), 2 * SSM_GROUP_DIM)
    bbar = bbar.reshape(SSM_HALVES, HALF_IN, 2 * HALF_LANES)
    cc = jnp.transpose(jnp.stack([c_re, -c_im]), (0, 1, 3, 2))
    cmat = _tile_block_diag(cc.reshape(STATE_LANES, SSM_GROUP_DIM), SSM_STATE)
    cmat = cmat.reshape(2, SSM_HALVES, HALF_LANES, HALF_IN)
    return abar, bbar, cmat


def _tile_block_diag(compact, rows_per_group):
    n_rows, c = compact.shape
    width = HALF_GROUPS * c
    src = lax.broadcasted_iota(jnp.int32, (c, width), 0)
    dst = lax.broadcasted_iota(jnp.int32, (c, width), 1)
    tiled = jnp.dot(compact, (src == dst % c).astype(F32))
    row_group = (lax.broadcasted_iota(jnp.int32, (n_rows, width), 0) // rows_per_group) % HALF_GROUPS
    col_group = lax.broadcasted_iota(jnp.int32, (n_rows, width), 1) // c
    return jnp.where(row_group == col_group, tiled, 0.0).astype(BF16)


def kernel(x_prompt, x_sample, p_prompt, p_sample, state_pool, state_ssm_re, state_ssm_im, g_mix, w_in, w_pool, pool_scale, lam_re, lam_im, log_dt, b_re, b_im, c_re, c_im, d_skip, w_glu_v, w_glu_g, w_out, g_ff, w_ff1, w_ff2, g_ple, w_ple, w_ple_gate, g_final):
    assert w_in.shape[0] == 1, "the final norm is fused into the (single) layer's ffn call"
    batch, seq_len, _ = x_prompt.shape
    dec_batch = x_sample.shape[0]

    abar, bbar, cmat = _s5_params(lam_re[0], lam_im[0], log_dt[0], b_re[0], b_im[0],
                                  c_re[0], c_im[0])
    mixer_vec = jnp.stack([g_mix[0], pool_scale[0],
                           jnp.pad(d_skip[0], (0, D_MODEL - SSM_WIDTH))])
    mixer_w = (mixer_vec, w_in[0].astype(BF16), w_pool[0].astype(BF16), abar, bbar, cmat,
               jnp.concatenate([w_glu_v[0], w_glu_g[0]], axis=1).astype(BF16),
               w_out[0].astype(BF16))
    ffn_w = (jnp.stack([g_ff[0], g_ple[0], g_final]), w_ff1[0].astype(BF16),
             w_ff2[0].astype(BF16), w_ple[0].astype(BF16), w_ple_gate[0].astype(BF16))

    x1p, pool_p, h_p = _mixer_call(x_prompt, None, None, mixer_w, n_seq=batch,
                                   t_chunk=PROMPT_T_CHUNK, pos_start=0)
    y_prompt = _ffn_call(x1p.reshape(batch * seq_len, D_MODEL),
                         p_prompt.reshape(batch * seq_len, PLE_DIM), ffn_w,
                         block_rows=FFN_BLOCK_ROWS).reshape(batch, seq_len, D_MODEL)
    pool_tm = jnp.swapaxes(state_pool[0], 0, 1).reshape(POOL_BUF * dec_batch, POOL_WIDTH)
    h0 = jnp.concatenate([state_ssm_re.reshape(dec_batch, PART_LANES),
                          state_ssm_im.reshape(dec_batch, PART_LANES)], axis=1)
    x1s, ua_s, h_s = _mixer_call(x_sample.reshape(1, dec_batch, D_MODEL), pool_tm, h0, mixer_w,
                                 n_seq=dec_batch, t_chunk=1, pos_start=PAST_LEN)
    y_sample = _ffn_call(x1s.reshape(dec_batch, D_MODEL), p_sample.reshape(dec_batch, PLE_DIM),
                         ffn_w, block_rows=dec_batch).reshape(dec_batch, 1, D_MODEL)

    new_pool_p = jnp.swapaxes(pool_p.reshape(POOL_BUF, batch, POOL_WIDTH), 0, 1)[None]
    new_pool_s = jnp.concatenate([state_pool[:, :, 1:], ua_s[None, :, None, :]], axis=2)
    state_shape = lambda n: (1, n, SSM_GROUPS, SSM_STATE)
    return (y_prompt, y_sample, new_pool_p, new_pool_s,
            h_p[:, :PART_LANES].reshape(state_shape(batch)),
            h_p[:, PART_LANES:].reshape(state_shape(batch)),
            h_s[:, :PART_LANES].reshape(state_shape(dec_batch)),
            h_s[:, PART_LANES:].reshape(state_shape(dec_batch)))
```

```python
import functools

import jax
import jax.numpy as jnp
import numpy as np
from jax import lax
from jax.experimental import pallas as pl
from jax.experimental.pallas import tpu as pltpu

D_MODEL = 1024
POOL_WINDOWS = (2, 4, 8, 16)
POOL_WIDTH = D_MODEL // 2
POOL_GROUP_DIM = POOL_WIDTH // len(POOL_WINDOWS)
POOL_OUT_GROUP_DIM = D_MODEL // len(POOL_WINDOWS)
POOL_BUF = max(POOL_WINDOWS) - 1
POOL_SLOTS = POOL_BUF + 1
SSM_WIDTH = D_MODEL // 2
SSM_GROUP_DIM = 16
SSM_GROUPS = SSM_WIDTH // SSM_GROUP_DIM
SSM_STATE = 64
PART_LANES = SSM_GROUPS * SSM_STATE
STATE_LANES = 2 * PART_LANES
SSM_HALVES = 2
HALF_IN = SSM_WIDTH // SSM_HALVES
HALF_GROUPS = SSM_GROUPS // SSM_HALVES
HALF_LANES = PART_LANES // SSM_HALVES
D_FF = 4 * D_MODEL
FF_CHUNK = 1024
PLE_DIM = 256
EPS = 1e-6
IN_WIDTH = POOL_WIDTH + SSM_WIDTH + 2 * D_MODEL
PAST_LEN = 16384
PROMPT_T_CHUNK = 64
PERM_T = 32
FFN_BLOCK_ROWS = 512
SCAN_LANES = 512
SCAN_UNROLL = 32
BF16_TILE_ROWS = 16
VMEM_LIMIT_BYTES = 58 * 1024 * 1024

BF16 = jnp.bfloat16
F32 = jnp.float32


def _rmsnorm(x, g):
    return x * lax.rsqrt(jnp.mean(x * x, axis=-1, keepdims=True) + EPS) * g


def _sigmoid(x):
    return 0.5 * jnp.tanh(0.5 * x) + 0.5


def _dot(a, b):
    return jnp.dot(a, b, preferred_element_type=F32)


def _mixer_kernel(*refs, n_seq, t_chunk, permute, pos_start, has_state):
    it = iter(refs)
    x_ref = next(it)
    pool_in_ref = next(it) if has_state else None
    h0_ref = next(it) if has_state else None
    perm_ref = next(it) if permute else None
    perm_t_ref = next(it) if permute else None
    vec_ref, w_in_ref, w_pool_ref, abar_ref, bbar_ref, cmat_ref, w_glu_ref, w_out_ref = (
        next(it) for _ in range(8))
    x1_ref, ua_out_ref, h_out_ref = (next(it) for _ in range(3))
    z_ref, ext_ref, hs_ref, h16_ref, hc_ref, ya_ref = (next(it) for _ in range(6))

    rows = n_seq * t_chunk
    step = pl.program_id(0)
    cur0 = POOL_SLOTS * n_seq
    g_mix = vec_ref[0:1, :]
    pool_scale = vec_ref[1:2, :]
    d_skip = vec_ref[2:3, 0:SSM_WIDTH]

    @pl.when(step == 0)
    def _init():
        if has_state:
            ext_ref[n_seq:cur0, :] = pool_in_ref[...]
            hc_ref[...] = h0_ref[...]
        else:
            ext_ref[0:cur0, :] = jnp.zeros((cur0, POOL_WIDTH), F32)
            hc_ref[...] = jnp.zeros((n_seq, STATE_LANES), F32)

    x = x_ref[...].reshape(rows, D_MODEL)
    h = _rmsnorm(x, g_mix).astype(BF16)
    if permute:
        blocks = []
        for j in range(t_chunk // PERM_T):
            picked = jnp.concatenate(
                [h[b * t_chunk + j * PERM_T:b * t_chunk + (j + 1) * PERM_T] for b in range(n_seq)],
                axis=0)
            blocks.append(_dot(perm_ref[...], picked).astype(BF16))
        h = jnp.concatenate(blocks, axis=0)
    z_ref[...] = _dot(h, w_in_ref[...])

    ext_ref[cur0:cur0 + rows, :] = z_ref[:, 0:POOL_WIDTH]
    t_idx = lax.shift_right_logical(
        lax.broadcasted_iota(jnp.int32, (rows, POOL_GROUP_DIM), 0), n_seq.bit_length() - 1)
    pos = t_idx + (pos_start + step * t_chunk)
    for gi, w in enumerate(POOL_WINDOWS):
        c0 = gi * POOL_GROUP_DIM
        cur = ext_ref[cur0:cur0 + rows, c0:c0 + POOL_GROUP_DIM]
        acc = cur
        for k in range(1, w):
            r0 = cur0 - k * n_seq
            acc = acc + ext_ref[r0:r0 + rows, c0:c0 + POOL_GROUP_DIM]
        count = jnp.minimum(pos + 1, w).astype(F32)
        pooled = (acc / count - cur).astype(BF16)
        o0 = gi * POOL_OUT_GROUP_DIM
        ya_ref[:, o0:o0 + POOL_OUT_GROUP_DIM] = (
            _dot(pooled, w_pool_ref[gi]) * pool_scale[:, o0:o0 + POOL_OUT_GROUP_DIM])
    keep = ua_out_ref.shape[0]
    ua_out_ref[...] = ext_ref[cur0 + rows - keep:cur0 + rows, :]
    if t_chunk >= POOL_SLOTS:
        ext_ref[0:cur0, :] = ext_ref[rows:rows + cur0, :]

    u_b = z_ref[:, POOL_WIDTH:POOL_WIDTH + SSM_WIDTH]
    u_b16 = u_b.astype(BF16)
    for hf in range(SSM_HALVES):
        bu = _dot(u_b16[:, hf * HALF_IN:(hf + 1) * HALF_IN], bbar_ref[hf])
        hs_ref[:, hf * HALF_LANES:(hf + 1) * HALF_LANES] = bu[:, 0:HALF_LANES]
        hs_ref[:, PART_LANES + hf * HALF_LANES:PART_LANES + (hf + 1) * HALF_LANES] = (
            bu[:, HALF_LANES:2 * HALF_LANES])

    steps_per_store = max(1, BF16_TILE_ROWS // n_seq)
    store_rows = steps_per_store * n_seq
    for q in range(PART_LANES // SCAN_LANES):
        re0 = q * SCAN_LANES
        im0 = PART_LANES + re0
        a_re = abar_ref[:, re0:re0 + SCAN_LANES]
        a_im = abar_ref[:, im0:im0 + SCAN_LANES]
        if t_chunk > 1:
            a_re = jnp.broadcast_to(a_re, (n_seq, SCAN_LANES))
            a_im = jnp.broadcast_to(a_im, (n_seq, SCAN_LANES))

        def scan_steps(i, carry, re0=re0, im0=im0, a_re=a_re, a_im=a_im):
            h_re, h_im = carry
            r0 = i * store_rows if isinstance(i, int) else pl.multiple_of(i * store_rows,
                                                                          store_rows)
            new_re, new_im = [], []
            for s in range(steps_per_store):
                rs = r0 + s * n_seq
                h_re, h_im = (
                    a_re * h_re - a_im * h_im + hs_ref[pl.ds(rs, n_seq), re0:re0 + SCAN_LANES],
                    a_re * h_im + a_im * h_re + hs_ref[pl.ds(rs, n_seq), im0:im0 + SCAN_LANES])
                new_re.append(h_re)
                new_im.append(h_im)
            h16_ref[pl.ds(r0, store_rows), re0:re0 + SCAN_LANES] = (
                jnp.concatenate(new_re, axis=0).astype(BF16))
            h16_ref[pl.ds(r0, store_rows), im0:im0 + SCAN_LANES] = (
                jnp.concatenate(new_im, axis=0).astype(BF16))
            return h_re, h_im

        carry = (hc_ref[:, re0:re0 + SCAN_LANES], hc_ref[:, im0:im0 + SCAN_LANES])
        if t_chunk == steps_per_store:
            carry = scan_steps(0, carry)
        else:
            carry = lax.fori_loop(0, t_chunk // steps_per_store, scan_steps, carry,
                                  unroll=SCAN_UNROLL)
        hc_ref[:, re0:re0 + SCAN_LANES] = carry[0]
        hc_ref[:, im0:im0 + SCAN_LANES] = carry[1]
    h_out_ref[...] = hc_ref[...]

    y_halves = []
    for hf in range(SSM_HALVES):
        l0 = hf * HALF_LANES
        y_halves.append(
            _dot(h16_ref[:, l0:l0 + HALF_LANES], cmat_ref[0, hf])
            + _dot(h16_ref[:, PART_LANES + l0:PART_LANES + l0 + HALF_LANES], cmat_ref[1, hf]))
    s = jnp.concatenate(y_halves, axis=-1) + d_skip * u_b

    gl = jax.nn.gelu(s).astype(BF16)
    vg = _dot(gl, w_glu_ref[...])
    branch_b = vg[:, 0:D_MODEL] * _sigmoid(vg[:, D_MODEL:2 * D_MODEL])
    g0 = POOL_WIDTH + SSM_WIDTH
    merged = (_sigmoid(z_ref[:, g0:g0 + D_MODEL]) * ya_ref[...]
              + _sigmoid(z_ref[:, g0 + D_MODEL:g0 + 2 * D_MODEL]) * branch_b).astype(BF16)
    if permute:
        block_rows = n_seq * PERM_T
        blocks = [_dot(perm_t_ref[...], merged[j * block_rows:(j + 1) * block_rows]).astype(BF16)
                  for j in range(t_chunk // PERM_T)]
        merged = jnp.concatenate(
            [blk[b * PERM_T:(b + 1) * PERM_T] for b in range(n_seq) for blk in blocks], axis=0)
    x1 = x + _dot(merged, w_out_ref[...])
    x1_ref[...] = x1.reshape(x1_ref.shape)


def _resident(shape):
    nd = len(shape)
    return pl.BlockSpec(shape, lambda i, _nd=nd: (0,) * _nd, pipeline_mode=pl.Buffered(1))


def _mixer_call(x, pool_in, h0, weights, *, n_seq, t_chunk, pos_start):
    rows = n_seq * t_chunk
    permute = t_chunk > 1
    has_state = pool_in is not None
    if permute:
        n_steps = x.shape[1] // t_chunk
        x_spec = pl.BlockSpec((n_seq, t_chunk, D_MODEL), lambda i: (0, i, 0))
    else:
        n_steps = 1
        x_spec = pl.BlockSpec((1, n_seq, D_MODEL), lambda i: (0, 0, 0))
    keep_rows = min(POOL_BUF, t_chunk) * n_seq

    in_arrays = [x]
    in_specs = [x_spec]
    if has_state:
        in_arrays += [pool_in, h0]
        in_specs += [_resident(pool_in.shape), _resident(h0.shape)]
    if permute:
        block_rows = n_seq * PERM_T
        r = np.arange(block_rows)
        perm = np.zeros((block_rows, block_rows), np.float32)
        perm[r, (r % n_seq) * PERM_T + r // n_seq] = 1.0
        in_arrays += [jnp.asarray(perm, BF16), jnp.asarray(perm.T, BF16)]
        in_specs += [_resident(perm.shape), _resident(perm.shape)]
    in_arrays += list(weights)
    in_specs += [_resident(w.shape) for w in weights]

    out_shape = (
        jax.ShapeDtypeStruct(x.shape, F32),
        jax.ShapeDtypeStruct((keep_rows, POOL_WIDTH), F32),
        jax.ShapeDtypeStruct((n_seq, STATE_LANES), F32),
    )
    out_specs = (
        x_spec,
        pl.BlockSpec((keep_rows, POOL_WIDTH), lambda i: (0, 0)),
        pl.BlockSpec((n_seq, STATE_LANES), lambda i: (0, 0)),
    )
    scratch = [
        pltpu.VMEM((rows, IN_WIDTH), F32),
        pltpu.VMEM(((POOL_SLOTS + t_chunk) * n_seq, POOL_WIDTH), F32),
        pltpu.VMEM((rows, STATE_LANES), F32),
        pltpu.VMEM((rows, STATE_LANES), BF16),
        pltpu.VMEM((n_seq, STATE_LANES), F32),
        pltpu.VMEM((rows, D_MODEL), F32),
    ]
    kern = functools.partial(_mixer_kernel, n_seq=n_seq, t_chunk=t_chunk, permute=permute,
                             pos_start=pos_start, has_state=has_state)
    return pl.pallas_call(
        kern,
        grid=(n_steps,),
        in_specs=in_specs,
        out_specs=out_specs,
        out_shape=out_shape,
        scratch_shapes=scratch,
        compiler_params=pltpu.CompilerParams(
            dimension_semantics=("arbitrary",), vmem_limit_bytes=VMEM_LIMIT_BYTES),
        name=f"mixer_n{n_seq}_t{t_chunk}",
    )(*in_arrays)


def _ffn_kernel(x1_ref, p_ref, vec_ref, w_ff1_ref, w_ff2_ref, w_ple_ref, w_ple_gate_ref, out_ref):
    x1 = x1_ref[...]
    h2 = _rmsnorm(x1, vec_ref[0:1, :]).astype(BF16)
    x2 = x1
    for c in range(D_FF // FF_CHUNK):
        hid = _dot(h2, w_ff1_ref[:, c * FF_CHUNK:(c + 1) * FF_CHUNK])
        hid = jnp.square(jnp.maximum(hid, 0.0)).astype(BF16)
        x2 = x2 + _dot(hid, w_ff2_ref[c * FF_CHUNK:(c + 1) * FF_CHUNK, :])
    h3 = _rmsnorm(x2, vec_ref[1:2, :]).astype(BF16)
    gate = _sigmoid(_dot(h3, w_ple_gate_ref[...]))
    x3 = x2 + _dot(p_ref[...].astype(BF16), w_ple_ref[...]) * gate
    out_ref[...] = _rmsnorm(x3, vec_ref[2:3, :])


def _ffn_call(x1, p, weights, *, block_rows):
    n_rows = x1.shape[0]
    block_rows = min(block_rows, n_rows)
    in_specs = [pl.BlockSpec((block_rows, D_MODEL), lambda i: (i, 0)),
                pl.BlockSpec((block_rows, PLE_DIM), lambda i: (i, 0))]
    in_specs += [_resident(w.shape) for w in weights]
    return pl.pallas_call(
        _ffn_kernel,
        grid=(n_rows // block_rows,),
        in_specs=in_specs,
        out_specs=pl.BlockSpec((block_rows, D_MODEL), lambda i: (i, 0)),
        out_shape=jax.ShapeDtypeStruct((n_rows, D_MODEL), F32),
        compiler_params=pltpu.CompilerParams(
            dimension_semantics=("parallel",), vmem_limit_bytes=VMEM_LIMIT_BYTES),
        name=f"ffn_r{n_rows}",
    )(x1, p, *weights)


def _s5_params(lam_re, lam_im, log_dt, b_re, b_im, c_re, c_im):
    dt = jnp.exp(log_dt)[:, None]
    mag = jnp.exp(lam_re * dt)
    ang = lam_im * dt
    abar_re = mag * jnp.cos(ang)
    abar_im = mag * jnp.sin(ang)
    den = lam_re * lam_re + lam_im * lam_im
    nr = abar_re - 1.0
    ni = abar_im
    k_re = ((nr * lam_re + ni * lam_im) / den)[:, :, None]
    k_im = ((ni * lam_re - nr * lam_im) / den)[:, :, None]
    bbar_re = k_re * b_re - k_im * b_im
    bbar_im = k_re * b_im + k_im * b_re
    abar = jnp.concatenate([abar_re.reshape(1, PART_LANES), abar_im.reshape(1, PART_LANES)],
                           axis=1)

    bb = jnp.transpose(jnp.stack([bbar_re, bbar_im]), (1, 3, 0, 2))
    bbar = _tile_block_diag(bb.reshape(2 * SSM_WIDTH, SSM_STATE), 2 * SSM_GROUP_DIM)
    bbar = bbar.reshape(SSM_HALVES, HALF_IN, 2 * HALF_LANES)
    cc = jnp.transpose(jnp.stack([c_re, -c_im]), (0, 1, 3, 2))
    cmat = _tile_block_diag(cc.reshape(STATE_LANES, SSM_GROUP_DIM), SSM_STATE)
    cmat = cmat.reshape(2, SSM_HALVES, HALF_LANES, HALF_IN)
    return abar, bbar, cmat


def _tile_block_diag(compact, rows_per_group):
    n_rows, c = compact.shape
    width = HALF_GROUPS * c
    src = lax.broadcasted_iota(jnp.int32, (c, width), 0)
    dst = lax.broadcasted_iota(jnp.int32, (c, width), 1)
    tiled = jnp.dot(compact, (src == dst % c).astype(F32))
    row_group = (lax.broadcasted_iota(jnp.int32, (n_rows, width), 0) // rows_per_group) % HALF_GROUPS
    col_group = lax.broadcasted_iota(jnp.int32, (n_rows, width), 1) // c
    return jnp.where(row_group == col_group, tiled, 0.0).astype(BF16)


def kernel(x_prompt, x_sample, p_prompt, p_sample, state_pool, state_ssm_re, state_ssm_im, g_mix, w_in, w_pool, pool_scale, lam_re, lam_im, log_dt, b_re, b_im, c_re, c_im, d_skip, w_glu_v, w_glu_g, w_out, g_ff, w_ff1, w_ff2, g_ple, w_ple, w_ple_gate, g_final):
    assert w_in.shape[0] == 1, "the final norm is fused into the (single) layer's ffn call"
    batch, seq_len, _ = x_prompt.shape
    dec_batch = x_sample.shape[0]

    abar, bbar, cmat = _s5_params(lam_re[0], lam_im[0], log_dt[0], b_re[0], b_im[0],
                                  c_re[0], c_im[0])
    mixer_vec = jnp.stack([g_mix[0], pool_scale[0],
                           jnp.pad(d_skip[0], (0, D_MODEL - SSM_WIDTH))])
    mixer_w = (mixer_vec, w_in[0].astype(BF16), w_pool[0].astype(BF16), abar, bbar, cmat,
               jnp.concatenate([w_glu_v[0], w_glu_g[0]], axis=1).astype(BF16),
               w_out[0].astype(BF16))
    ffn_w = (jnp.stack([g_ff[0], g_ple[0], g_final]), w_ff1[0].astype(BF16),
             w_ff2[0].astype(BF16), w_ple[0].astype(BF16), w_ple_gate[0].astype(BF16))

    x1p, pool_p, h_p = _mixer_call(x_prompt, None, None, mixer_w, n_seq=batch,
                                   t_chunk=PROMPT_T_CHUNK, pos_start=0)
    y_prompt = _ffn_call(x1p.reshape(batch * seq_len, D_MODEL),
                         p_prompt.reshape(batch * seq_len, PLE_DIM), ffn_w,
                         block_rows=FFN_BLOCK_ROWS).reshape(batch, seq_len, D_MODEL)
    pool_tm = jnp.swapaxes(state_pool[0], 0, 1).reshape(POOL_BUF * dec_batch, POOL_WIDTH)
    h0 = jnp.concatenate([state_ssm_re.reshape(dec_batch, PART_LANES),
                          state_ssm_im.reshape(dec_batch, PART_LANES)], axis=1)
    x1s, ua_s, h_s = _mixer_call(x_sample.reshape(1, dec_batch, D_MODEL), pool_tm, h0, mixer_w,
                                 n_seq=dec_batch, t_chunk=1, pos_start=PAST_LEN)
    y_sample = _ffn_call(x1s.reshape(dec_batch, D_MODEL), p_sample.reshape(dec_batch, PLE_DIM),
                         ffn_w, block_rows=dec_batch).reshape(dec_batch, 1, D_MODEL)

    new_pool_p = jnp.swapaxes(pool_p.reshape(POOL_BUF, batch, POOL_WIDTH), 0, 1)[None]
    new_pool_s = jnp.concatenate([state_pool[:, :, 1:], ua_s[None, :, None, :]], axis=2)
    state_shape = lambda n: (1, n, SSM_GROUPS, SSM_STATE)
    return (y_prompt, y_sample, new_pool_p, new_pool_s,
            h_p[:, :PART_LANES].reshape(state_shape(batch)),
            h_p[:, PART_LANES:].reshape(state_shape(batch)),
            h_s[:, :PART_LANES].reshape(state_shape(dec_batch)),
            h_s[:, PART_LANES:].reshape(state_shape(dec_batch)))
```

```python
import functools

import jax
import jax.numpy as jnp
import numpy as np
from jax import lax
from jax.experimental import pallas as pl
from jax.experimental.pallas import tpu as pltpu

D_MODEL = 1024
POOL_WINDOWS = (2, 4, 8, 16)
POOL_WIDTH = D_MODEL // 2
POOL_GROUP_DIM = POOL_WIDTH // len(POOL_WINDOWS)
POOL_OUT_GROUP_DIM = D_MODEL // len(POOL_WINDOWS)
POOL_BUF = max(POOL_WINDOWS) - 1
POOL_SLOTS = POOL_BUF + 1
SSM_WIDTH = D_MODEL // 2
SSM_GROUP_DIM = 16
SSM_GROUPS = SSM_WIDTH // SSM_GROUP_DIM
SSM_STATE = 64
PART_LANES = SSM_GROUPS * SSM_STATE
STATE_LANES = 2 * PART_LANES
SSM_HALVES = 2
HALF_IN = SSM_WIDTH // SSM_HALVES
HALF_GROUPS = SSM_GROUPS // SSM_HALVES
HALF_LANES = PART_LANES // SSM_HALVES
D_FF = 4 * D_MODEL
FF_CHUNK = 1024
PLE_DIM = 256
EPS = 1e-6
IN_WIDTH = POOL_WIDTH + SSM_WIDTH + 2 * D_MODEL
PAST_LEN = 16384
PROMPT_T_CHUNK = 64
PERM_T = 32
FFN_BLOCK_ROWS = 1024
SCAN_LANES = 512
SCAN_UNROLL = 32
BF16_TILE_ROWS = 16
VMEM_LIMIT_BYTES = 58 * 1024 * 1024

VEC_G_MIX, VEC_POOL_SCALE, VEC_D_SKIP, VEC_G_FF, VEC_G_PLE, VEC_G_FINAL = range(6)
S5_B_RE, S5_B_IM, S5_C_RE, S5_C_IM_NEG = range(4)

BF16 = jnp.bfloat16
F32 = jnp.float32


def _rmsnorm(x, g):
    return x * lax.rsqrt(jnp.mean(x * x, axis=-1, keepdims=True) + EPS) * g


def _sigmoid(x):
    return 0.5 * jnp.tanh(0.5 * x) + 0.5


def _dot(a, b):
    return jnp.dot(a, b, preferred_element_type=F32)


def _dot_nt(a, b):
    return lax.dot_general(a, b, (((1,), (1,)), ((), ())), preferred_element_type=F32)


def _mixer_kernel(*refs, n_seq, t_chunk, permute, pos_start, has_state):
    it = iter(refs)
    x_ref = next(it)
    pool_in_ref = next(it) if has_state else None
    h0_ref = next(it) if has_state else None
    perm_ref = next(it) if permute else None
    perm_t_ref = next(it) if permute else None
    (vec_ref, w_in_ref, w_pool_ref, abar_ref, s5mat_ref, w_glu_v_ref, w_glu_g_ref,
     w_out_ref) = (next(it) for _ in range(8))
    x1_ref, ua_out_ref, h_re_out_ref, h_im_out_ref = (next(it) for _ in range(4))
    gates_ref, ub_ref, ext_ref, hs_ref, h16_ref, hc_ref, ya_ref = (next(it) for _ in range(7))

    rows = n_seq * t_chunk
    step = pl.program_id(0)
    cur0 = POOL_SLOTS * n_seq
    g_mix = vec_ref[VEC_G_MIX:VEC_G_MIX + 1, :]
    pool_scale = vec_ref[VEC_POOL_SCALE:VEC_POOL_SCALE + 1, :]
    d_skip = vec_ref[VEC_D_SKIP:VEC_D_SKIP + 1, 0:SSM_WIDTH]

    @pl.when(step == 0)
    def _init():
        if has_state:
            ext_ref[n_seq:cur0, :] = pool_in_ref[...]
            hc_ref[...] = h0_ref[...]
        else:
            ext_ref[0:cur0, :] = jnp.zeros((cur0, POOL_WIDTH), F32)
            hc_ref[...] = jnp.zeros((n_seq, STATE_LANES), F32)

    x = x_ref[...].reshape(rows, D_MODEL)
    h = _rmsnorm(x, g_mix).astype(BF16)
    if permute:
        blocks = []
        for j in range(t_chunk // PERM_T):
            picked = jnp.concatenate(
                [h[b * t_chunk + j * PERM_T:b * t_chunk + (j + 1) * PERM_T] for b in range(n_seq)],
                axis=0)
            blocks.append(_dot(perm_ref[...], picked).astype(BF16))
        h = jnp.concatenate(blocks, axis=0)
    g0 = POOL_WIDTH + SSM_WIDTH
    u = _dot(h, w_in_ref[:, 0:g0])
    ext_ref[cur0:cur0 + rows, :] = u[:, 0:POOL_WIDTH]
    ub_ref[...] = u[:, POOL_WIDTH:g0]
    u_b16 = u[:, POOL_WIDTH:g0].astype(BF16)
    for hf in range(SSM_HALVES):
        u_half = u_b16[:, hf * HALF_IN:(hf + 1) * HALF_IN]
        l0 = hf * HALF_LANES
        hs_ref[:, l0:l0 + HALF_LANES] = _dot(u_half, s5mat_ref[S5_B_RE, hf])
        hs_ref[:, PART_LANES + l0:PART_LANES + l0 + HALF_LANES] = _dot(
            u_half, s5mat_ref[S5_B_IM, hf])
    gates_ref[...] = _dot(h, w_in_ref[:, g0:IN_WIDTH])

    t_idx = lax.shift_right_logical(
        lax.broadcasted_iota(jnp.int32, (rows, POOL_GROUP_DIM), 0), n_seq.bit_length() - 1)
    pos = t_idx + (pos_start + step * t_chunk)
    for gi, w in enumerate(POOL_WINDOWS):
        c0 = gi * POOL_GROUP_DIM
        cur = ext_ref[cur0:cur0 + rows, c0:c0 + POOL_GROUP_DIM]
        acc = cur
        for k in range(1, w):
            r0 = cur0 - k * n_seq
            acc = acc + ext_ref[r0:r0 + rows, c0:c0 + POOL_GROUP_DIM]
        count = jnp.minimum(pos + 1, w).astype(F32)
        pooled = (acc / count - cur).astype(BF16)
        o0 = gi * POOL_OUT_GROUP_DIM
        ya_ref[:, o0:o0 + POOL_OUT_GROUP_DIM] = (
            _dot(pooled, w_pool_ref[gi]) * pool_scale[:, o0:o0 + POOL_OUT_GROUP_DIM])
    ua_out_ref[...] = ext_ref[cur0 + rows - POOL_BUF * n_seq:cur0 + rows, :]
    if t_chunk >= POOL_SLOTS:
        ext_ref[0:cur0, :] = ext_ref[rows:rows + cur0, :]

    steps_per_store = max(1, BF16_TILE_ROWS // n_seq)
    store_rows = steps_per_store * n_seq
    for q in range(PART_LANES // SCAN_LANES):
        re0 = q * SCAN_LANES
        im0 = PART_LANES + re0
        a_re = abar_ref[:, re0:re0 + SCAN_LANES]
        a_im = abar_ref[:, im0:im0 + SCAN_LANES]
        if t_chunk > 1:
            a_re = jnp.broadcast_to(a_re, (n_seq, SCAN_LANES))
            a_im = jnp.broadcast_to(a_im, (n_seq, SCAN_LANES))

        def scan_steps(i, carry, re0=re0, im0=im0, a_re=a_re, a_im=a_im):
            h_re, h_im = carry
            r0 = i * store_rows if isinstance(i, int) else pl.multiple_of(i * store_rows,
                                                                          store_rows)
            new_re, new_im = [], []
            for s in range(steps_per_store):
                rs = r0 + s * n_seq
                h_re, h_im = (
                    a_re * h_re - a_im * h_im + hs_ref[pl.ds(rs, n_seq), re0:re0 + SCAN_LANES],
                    a_re * h_im + a_im * h_re + hs_ref[pl.ds(rs, n_seq), im0:im0 + SCAN_LANES])
                new_re.append(h_re)
                new_im.append(h_im)
            h16_ref[pl.ds(r0, store_rows), re0:re0 + SCAN_LANES] = (
                jnp.concatenate(new_re, axis=0).astype(BF16))
            h16_ref[pl.ds(r0, store_rows), im0:im0 + SCAN_LANES] = (
                jnp.concatenate(new_im, axis=0).astype(BF16))
            return h_re, h_im

        carry = (hc_ref[:, re0:re0 + SCAN_LANES], hc_ref[:, im0:im0 + SCAN_LANES])
        if t_chunk == steps_per_store:
            carry = scan_steps(0, carry)
        else:
            carry = lax.fori_loop(0, t_chunk // steps_per_store, scan_steps, carry,
                                  unroll=SCAN_UNROLL)
        hc_ref[:, re0:re0 + SCAN_LANES] = carry[0]
        hc_ref[:, im0:im0 + SCAN_LANES] = carry[1]
    h_re_out_ref[...] = hc_ref[:, 0:PART_LANES]
    h_im_out_ref[...] = hc_ref[:, PART_LANES:STATE_LANES]

    y_halves = []
    for hf in range(SSM_HALVES):
        l0 = hf * HALF_LANES
        y_halves.append(
            _dot_nt(h16_ref[:, l0:l0 + HALF_LANES], s5mat_ref[S5_C_RE, hf])
            + _dot_nt(h16_ref[:, PART_LANES + l0:PART_LANES + l0 + HALF_LANES],
                      s5mat_ref[S5_C_IM_NEG, hf]))
    s = jnp.concatenate(y_halves, axis=-1) + d_skip * ub_ref[...]

    gl = jax.nn.gelu(s).astype(BF16)
    branch_b = _dot(gl, w_glu_v_ref[...]) * _sigmoid(_dot(gl, w_glu_g_ref[...]))
    merged = (_sigmoid(gates_ref[:, 0:D_MODEL]) * ya_ref[...]
              + _sigmoid(gates_ref[:, D_MODEL:2 * D_MODEL]) * branch_b).astype(BF16)
    if permute:
        block_rows = n_seq * PERM_T
        blocks = [_dot(perm_t_ref[...], merged[j * block_rows:(j + 1) * block_rows]).astype(BF16)
                  for j in range(t_chunk // PERM_T)]
        merged = jnp.concatenate(
            [blk[b * PERM_T:(b + 1) * PERM_T] for b in range(n_seq) for blk in blocks], axis=0)
    x1 = x + _dot(merged, w_out_ref[...])
    x1_ref[...] = x1.reshape(x1_ref.shape)


def _resident(shape):
    nd = len(shape)
    return pl.BlockSpec(shape, lambda i, _nd=nd: (0,) * _nd, pipeline_mode=pl.Buffered(1))


def _mixer_call(x, pool_in, h0, weights, *, n_seq, t_chunk, pos_start):
    rows = n_seq * t_chunk
    permute = t_chunk > 1
    has_state = pool_in is not None
    if permute:
        n_steps = x.shape[1] // t_chunk
        x_spec = pl.BlockSpec((n_seq, t_chunk, D_MODEL), lambda i: (0, i, 0))
    else:
        n_steps = 1
        x_spec = pl.BlockSpec((1, n_seq, D_MODEL), lambda i: (0, 0, 0))
    keep_rows = POOL_BUF * n_seq

    in_arrays = [x]
    in_specs = [x_spec]
    if has_state:
        in_arrays += [pool_in, h0]
        in_specs += [_resident(pool_in.shape), _resident(h0.shape)]
    if permute:
        block_rows = n_seq * PERM_T
        r = np.arange(block_rows)
        perm = np.zeros((block_rows, block_rows), np.float32)
        perm[r, (r % n_seq) * PERM_T + r // n_seq] = 1.0
        in_arrays += [jnp.asarray(perm, BF16), jnp.asarray(perm.T, BF16)]
        in_specs += [_resident(perm.shape), _resident(perm.shape)]
    in_arrays += list(weights)
    in_specs += [_resident(w.shape) for w in weights]

    out_shape = (
        jax.ShapeDtypeStruct(x.shape, F32),
        jax.ShapeDtypeStruct((keep_rows, POOL_WIDTH), F32),
        jax.ShapeDtypeStruct((n_seq, PART_LANES), F32),
        jax.ShapeDtypeStruct((n_seq, PART_LANES), F32),
    )
    out_specs = (
        x_spec,
        pl.BlockSpec((keep_rows, POOL_WIDTH), lambda i: (0, 0)),
        pl.BlockSpec((n_seq, PART_LANES), lambda i: (0, 0)),
        pl.BlockSpec((n_seq, PART_LANES), lambda i: (0, 0)),
    )
    scratch = [
        pltpu.VMEM((rows, 2 * D_MODEL), F32),
        pltpu.VMEM((rows, SSM_WIDTH), F32),
        pltpu.VMEM(((POOL_SLOTS + t_chunk) * n_seq, POOL_WIDTH), F32),
        pltpu.VMEM((rows, STATE_LANES), F32),
        pltpu.VMEM((rows, STATE_LANES), BF16),
        pltpu.VMEM((n_seq, STATE_LANES), F32),
        pltpu.VMEM((rows, D_MODEL), F32),
    ]
    kern = functools.partial(_mixer_kernel, n_seq=n_seq, t_chunk=t_chunk, permute=permute,
                             pos_start=pos_start, has_state=has_state)
    return pl.pallas_call(
        kern,
        grid=(n_steps,),
        in_specs=in_specs,
        out_specs=out_specs,
        out_shape=out_shape,
        scratch_shapes=scratch,
        compiler_params=pltpu.CompilerParams(
            dimension_semantics=("arbitrary",), vmem_limit_bytes=VMEM_LIMIT_BYTES),
        name=f"mixer_n{n_seq}_t{t_chunk}",
    )(*in_arrays)


def _ffn_kernel(x1_ref, p_ref, vec_ref, w_ff1_ref, w_ff2_ref, w_ple_ref, w_ple_gate_ref, out_ref):
    x1 = x1_ref[...]
    h2 = _rmsnorm(x1, vec_ref[VEC_G_FF:VEC_G_FF + 1, :]).astype(BF16)
    x2 = x1
    for c in range(D_FF // FF_CHUNK):
        hid = _dot(h2, w_ff1_ref[:, c * FF_CHUNK:(c + 1) * FF_CHUNK])
        hid = jnp.square(jnp.maximum(hid, 0.0)).astype(BF16)
        x2 = x2 + _dot(hid, w_ff2_ref[c * FF_CHUNK:(c + 1) * FF_CHUNK, :])
    h3 = _rmsnorm(x2, vec_ref[VEC_G_PLE:VEC_G_PLE + 1, :]).astype(BF16)
    gate = _sigmoid(_dot(h3, w_ple_gate_ref[...]))
    x3 = x2 + _dot(p_ref[...].astype(BF16), w_ple_ref[...]) * gate
    out_ref[...] = _rmsnorm(x3, vec_ref[VEC_G_FINAL:VEC_G_FINAL + 1, :])


def _ffn_call(x1, p, weights, *, block_rows):
    n_rows = x1.shape[0]
    block_rows = min(block_rows, n_rows)
    in_specs = [pl.BlockSpec((block_rows, D_MODEL), lambda i: (i, 0)),
                pl.BlockSpec((block_rows, PLE_DIM), lambda i: (i, 0))]
    in_specs += [_resident(w.shape) for w in weights]
    return pl.pallas_call(
        _ffn_kernel,
        grid=(n_rows // block_rows,),
        in_specs=in_specs,
        out_specs=pl.BlockSpec((block_rows, D_MODEL), lambda i: (i, 0)),
        out_shape=jax.ShapeDtypeStruct((n_rows, D_MODEL), F32),
        compiler_params=pltpu.CompilerParams(
            dimension_semantics=("parallel",), vmem_limit_bytes=VMEM_LIMIT_BYTES),
        name=f"ffn_r{n_rows}",
    )(x1, p, *weights)


def _s5_params(lam_re, lam_im, log_dt, b_re, b_im, c_re, c_im):
    dt = jnp.exp(log_dt)[:, None]
    mag = jnp.exp(lam_re * dt)
    ang = lam_im * dt
    abar_re = mag * jnp.cos(ang)
    abar_im = mag * jnp.sin(ang)
    den = lam_re * lam_re + lam_im * lam_im
    nr = abar_re - 1.0
    ni = abar_im
    k_re = ((nr * lam_re + ni * lam_im) / den)[:, None, :]
    k_im = ((ni * lam_re - nr * lam_im) / den)[:, None, :]
    b_re_t = jnp.swapaxes(b_re, 1, 2)
    b_im_t = jnp.swapaxes(b_im, 1, 2)
    bbar_re = k_re * b_re_t - k_im * b_im_t
    bbar_im = k_re * b_im_t + k_im * b_re_t
    abar = jnp.concatenate([abar_re.reshape(1, PART_LANES), abar_im.reshape(1, PART_LANES)],
                           axis=1)

    compact = jnp.stack([bbar_re, bbar_im, c_re, -c_im]).reshape(4 * SSM_WIDTH, SSM_STATE)
    src = lax.broadcasted_iota(jnp.int32, (SSM_STATE, HALF_LANES), 0)
    dst = lax.broadcasted_iota(jnp.int32, (SSM_STATE, HALF_LANES), 1)
    tiled = jnp.dot(compact, (src == dst % SSM_STATE).astype(F32))
    row_group = (lax.broadcasted_iota(jnp.int32, tiled.shape, 0) // SSM_GROUP_DIM) % HALF_GROUPS
    col_group = lax.broadcasted_iota(jnp.int32, tiled.shape, 1) // SSM_STATE
    s5mat = jnp.where(row_group == col_group, tiled, 0.0).astype(BF16).reshape(
        4, SSM_HALVES, HALF_IN, HALF_LANES)
    return abar, s5mat


def kernel(x_prompt, x_sample, p_prompt, p_sample, state_pool, state_ssm_re, state_ssm_im, g_mix, w_in, w_pool, pool_scale, lam_re, lam_im, log_dt, b_re, b_im, c_re, c_im, d_skip, w_glu_v, w_glu_g, w_out, g_ff, w_ff1, w_ff2, g_ple, w_ple, w_ple_gate, g_final):
    assert w_in.shape[0] == 1, "the final norm is fused into the (single) layer's ffn call"
    batch, seq_len, _ = x_prompt.shape
    dec_batch = x_sample.shape[0]

    abar, s5mat = _s5_params(lam_re[0], lam_im[0], log_dt[0], b_re[0], b_im[0], c_re[0], c_im[0])
    vec = jnp.stack([g_mix[0], pool_scale[0], jnp.concatenate([d_skip[0], d_skip[0]]),
                     g_ff[0], g_ple[0], g_final])
    mixer_w = (vec, w_in[0].astype(BF16), w_pool[0].astype(BF16), abar, s5mat,
               w_glu_v[0].astype(BF16), w_glu_g[0].astype(BF16), w_out[0].astype(BF16))
    ffn_w = (vec, w_ff1[0].astype(BF16), w_ff2[0].astype(BF16), w_ple[0].astype(BF16),
             w_ple_gate[0].astype(BF16))

    x1p, pool_p, re_p, im_p = _mixer_call(x_prompt, None, None, mixer_w, n_seq=batch,
                                          t_chunk=PROMPT_T_CHUNK, pos_start=0)
    y_prompt = _ffn_call(x1p.reshape(batch * seq_len, D_MODEL),
                         p_prompt.reshape(batch * seq_len, PLE_DIM), ffn_w,
                         block_rows=FFN_BLOCK_ROWS).reshape(batch, seq_len, D_MODEL)
    pool_tm = jnp.swapaxes(state_pool[0], 0, 1).reshape(POOL_BUF * dec_batch, POOL_WIDTH)
    h0 = jnp.concatenate([state_ssm_re.reshape(dec_batch, PART_LANES),
                          state_ssm_im.reshape(dec_batch, PART_LANES)], axis=1)
    x1s, pool_s, re_s, im_s = _mixer_call(
        x_sample.reshape(1, dec_batch, D_MODEL), pool_tm, h0, mixer_w, n_seq=dec_batch,
        t_chunk=1, pos_start=PAST_LEN)
    y_sample = _ffn_call(x1s.reshape(dec_batch, D_MODEL), p_sample.reshape(dec_batch, PLE_DIM),
                         ffn_w, block_rows=dec_batch).reshape(dec_batch, 1, D_MODEL)

    def to_buf(tm, n):
        return jnp.swapaxes(tm.reshape(POOL_BUF, n, POOL_WIDTH), 0, 1)[None]

    state_shape = lambda n: (1, n, SSM_GROUPS, SSM_STATE)
    return (y_prompt, y_sample, to_buf(pool_p, batch), to_buf(pool_s, dec_batch),
            re_p.reshape(state_shape(batch)), im_p.reshape(state_shape(batch)),
            re_s.reshape(state_shape(dec_batch)), im_s.reshape(state_shape(dec_batch)))
```

```python
import functools

import jax
import jax.numpy as jnp
import numpy as np
from jax import lax
from jax.experimental import pallas as pl
from jax.experimental.pallas import tpu as pltpu

D_MODEL = 1024
POOL_WINDOWS = (2, 4, 8, 16)
POOL_WIDTH = D_MODEL // 2
POOL_GROUP_DIM = POOL_WIDTH // len(POOL_WINDOWS)
POOL_OUT_GROUP_DIM = D_MODEL // len(POOL_WINDOWS)
POOL_BUF = max(POOL_WINDOWS) - 1
POOL_SLOTS = POOL_BUF + 1
SSM_WIDTH = D_MODEL // 2
SSM_GROUP_DIM = 16
SSM_GROUPS = SSM_WIDTH // SSM_GROUP_DIM
SSM_STATE = 64
PART_LANES = SSM_GROUPS * SSM_STATE
STATE_LANES = 2 * PART_LANES
SSM_HALVES = 2
HALF_IN = SSM_WIDTH // SSM_HALVES
HALF_GROUPS = SSM_GROUPS // SSM_HALVES
HALF_LANES = PART_LANES // SSM_HALVES
D_FF = 4 * D_MODEL
FF_CHUNK = 1024
PLE_DIM = 256
EPS = 1e-6
IN_WIDTH = POOL_WIDTH + SSM_WIDTH + 2 * D_MODEL
PAST_LEN = 16384
PROMPT_T_CHUNK = 64
PERM_T = 32
FFN_BLOCK_ROWS = 1024
SCAN_LANES = 512
SCAN_UNROLL = 32
BF16_TILE_ROWS = 16
CAST_ROWS, CAST_COLS = 512, 1024
VMEM_LIMIT_BYTES = 58 * 1024 * 1024

VEC_G_MIX, VEC_POOL_SCALE, VEC_D_SKIP, VEC_G_FF, VEC_G_PLE, VEC_G_FINAL = range(6)
S5_B_RE, S5_B_IM, S5_C_RE, S5_C_IM_NEG = range(4)

BF16 = jnp.bfloat16
F32 = jnp.float32


def _rmsnorm(x, g):
    return x * lax.rsqrt(jnp.mean(x * x, axis=-1, keepdims=True) + EPS) * g


def _sigmoid(x):
    return 0.5 * jnp.tanh(0.5 * x) + 0.5


def _dot(a, b):
    return jnp.dot(a, b, preferred_element_type=F32)


def _dot_nt(a, b):
    return lax.dot_general(a, b, (((1,), (1,)), ((), ())), preferred_element_type=F32)


def _resident(shape):
    nd = len(shape)
    return pl.BlockSpec(shape, lambda i, _nd=nd: (0,) * _nd, pipeline_mode=pl.Buffered(1))


_IN_HBM = pl.BlockSpec(memory_space=pl.ANY)


def _stream_cast(pairs, stage_ref, sem):
    jobs = []
    for src, dst in pairs:
        n_rows, n_cols = src.shape
        rows = min(CAST_ROWS, n_rows)
        jobs += [(src, dst, r0, rows, c0)
                 for r0 in range(0, n_rows, rows) for c0 in range(0, n_cols, CAST_COLS)]

    def copy(k):
        src, _, r0, rows, c0 = jobs[k]
        return pltpu.make_async_copy(src.at[r0:r0 + rows, c0:c0 + CAST_COLS],
                                     stage_ref.at[k % 2, 0:rows], sem.at[k % 2])

    copy(0).start()
    for k, (_, dst, r0, rows, c0) in enumerate(jobs):
        if k + 1 < len(jobs):
            copy(k + 1).start()
        copy(k).wait()
        dst[r0:r0 + rows, c0:c0 + CAST_COLS] = stage_ref[k % 2, 0:rows].astype(BF16)


def _mixer_kernel(*refs, n_seq, t_chunk, permute, pos_start, has_state, cast_weights):
    it = iter(refs)
    x_ref = next(it)
    pool_in_ref = next(it) if has_state else None
    h0_ref = next(it) if has_state else None
    perm_ref = next(it) if permute else None
    perm_t_ref = next(it) if permute else None
    vec_ref, w_pool_ref, abar_ref, s5mat_ref = (next(it) for _ in range(4))
    big_in = [next(it) for _ in range(4)]
    x1_ref, ua_out_ref, h_re_out_ref, h_im_out_ref = (next(it) for _ in range(4))
    big_out = [next(it) for _ in range(4)] if cast_weights else None
    gates_ref, ub_ref, ext_ref, hs_ref, h16_ref, hc_ref, ya_ref = (next(it) for _ in range(7))
    if cast_weights:
        big_vmem = [next(it) for _ in range(4)]
        stage_ref, cast_sem, export_sem = (next(it) for _ in range(3))
    else:
        big_vmem = big_in
    w_in_ref, w_glu_v_ref, w_glu_g_ref, w_out_ref = big_vmem

    rows = n_seq * t_chunk
    step = pl.program_id(0)
    cur0 = POOL_SLOTS * n_seq
    g_mix = vec_ref[VEC_G_MIX:VEC_G_MIX + 1, :]
    pool_scale = vec_ref[VEC_POOL_SCALE:VEC_POOL_SCALE + 1, :]
    d_skip = vec_ref[VEC_D_SKIP:VEC_D_SKIP + 1, 0:SSM_WIDTH]

    def export_copy(i):
        return pltpu.make_async_copy(big_vmem[i], big_out[i], export_sem.at[i])

    @pl.when(step == 0)
    def _init():
        if has_state:
            ext_ref[n_seq:cur0, :] = pool_in_ref[...]
            hc_ref[...] = h0_ref[...]
        else:
            ext_ref[0:cur0, :] = jnp.zeros((cur0, POOL_WIDTH), F32)
            hc_ref[...] = jnp.zeros((n_seq, STATE_LANES), F32)
        if cast_weights:
            _stream_cast(list(zip(big_in, big_vmem)), stage_ref, cast_sem)
            for i in range(len(big_vmem)):
                export_copy(i).start()

    x = x_ref[...].reshape(rows, D_MODEL)
    h = _rmsnorm(x, g_mix).astype(BF16)
    if permute:
        blocks = []
        for j in range(t_chunk // PERM_T):
            picked = jnp.concatenate(
                [h[b * t_chunk + j * PERM_T:b * t_chunk + (j + 1) * PERM_T] for b in range(n_seq)],
                axis=0)
            blocks.append(_dot(perm_ref[...], picked).astype(BF16))
        h = jnp.concatenate(blocks, axis=0)
    g0 = POOL_WIDTH + SSM_WIDTH
    u = _dot(h, w_in_ref[:, 0:g0])
    ext_ref[cur0:cur0 + rows, :] = u[:, 0:POOL_WIDTH]
    ub_ref[...] = u[:, POOL_WIDTH:g0]
    u_b16 = u[:, POOL_WIDTH:g0].astype(BF16)
    for hf in range(SSM_HALVES):
        u_half = u_b16[:, hf * HALF_IN:(hf + 1) * HALF_IN]
        l0 = hf * HALF_LANES
        hs_ref[:, l0:l0 + HALF_LANES] = _dot(u_half, s5mat_ref[S5_B_RE, hf])
        hs_ref[:, PART_LANES + l0:PART_LANES + l0 + HALF_LANES] = _dot(
            u_half, s5mat_ref[S5_B_IM, hf])
    gates_ref[...] = _dot(h, w_in_ref[:, g0:IN_WIDTH])

    t_idx = lax.shift_right_logical(
        lax.broadcasted_iota(jnp.int32, (rows, POOL_GROUP_DIM), 0), n_seq.bit_length() - 1)
    pos = t_idx + (pos_start + step * t_chunk)
    for gi, w in enumerate(POOL_WINDOWS):
        c0 = gi * POOL_GROUP_DIM
        cur = ext_ref[cur0:cur0 + rows, c0:c0 + POOL_GROUP_DIM]
        acc = cur
        for k in range(1, w):
            r0 = cur0 - k * n_seq
            acc = acc + ext_ref[r0:r0 + rows, c0:c0 + POOL_GROUP_DIM]
        count = jnp.minimum(pos + 1, w).astype(F32)
        pooled = (acc / count - cur).astype(BF16)
        o0 = gi * POOL_OUT_GROUP_DIM
        ya_ref[:, o0:o0 + POOL_OUT_GROUP_DIM] = (
            _dot(pooled, w_pool_ref[gi]) * pool_scale[:, o0:o0 + POOL_OUT_GROUP_DIM])
    ua_out_ref[...] = ext_ref[cur0 + rows - POOL_BUF * n_seq:cur0 + rows, :]
    if t_chunk >= POOL_SLOTS:
        ext_ref[0:cur0, :] = ext_ref[rows:rows + cur0, :]

    steps_per_store = max(1, BF16_TILE_ROWS // n_seq)
    store_rows = steps_per_store * n_seq
    for q in range(PART_LANES // SCAN_LANES):
        re0 = q * SCAN_LANES
        im0 = PART_LANES + re0
        a_re = abar_ref[:, re0:re0 + SCAN_LANES]
        a_im = abar_ref[:, im0:im0 + SCAN_LANES]
        if t_chunk > 1:
            a_re = jnp.broadcast_to(a_re, (n_seq, SCAN_LANES))
            a_im = jnp.broadcast_to(a_im, (n_seq, SCAN_LANES))

        def scan_steps(i, carry, re0=re0, im0=im0, a_re=a_re, a_im=a_im):
            h_re, h_im = carry
            r0 = i * store_rows if isinstance(i, int) else pl.multiple_of(i * store_rows,
                                                                          store_rows)
            new_re, new_im = [], []
            for s in range(steps_per_store):
                rs = r0 + s * n_seq
                h_re, h_im = (
                    a_re * h_re - a_im * h_im + hs_ref[pl.ds(rs, n_seq), re0:re0 + SCAN_LANES],
                    a_re * h_im + a_im * h_re + hs_ref[pl.ds(rs, n_seq), im0:im0 + SCAN_LANES])
                new_re.append(h_re)
                new_im.append(h_im)
            h16_ref[pl.ds(r0, store_rows), re0:re0 + SCAN_LANES] = (
                jnp.concatenate(new_re, axis=0).astype(BF16))
            h16_ref[pl.ds(r0, store_rows), im0:im0 + SCAN_LANES] = (
                jnp.concatenate(new_im, axis=0).astype(BF16))
            return h_re, h_im

        carry = (hc_ref[:, re0:re0 + SCAN_LANES], hc_ref[:, im0:im0 + SCAN_LANES])
        if t_chunk == steps_per_store:
            carry = scan_steps(0, carry)
        else:
            carry = lax.fori_loop(0, t_chunk // steps_per_store, scan_steps, carry,
                                  unroll=SCAN_UNROLL)
        hc_ref[:, re0:re0 + SCAN_LANES] = carry[0]
        hc_ref[:, im0:im0 + SCAN_LANES] = carry[1]
    h_re_out_ref[...] = hc_ref[:, 0:PART_LANES]
    h_im_out_ref[...] = hc_ref[:, PART_LANES:STATE_LANES]

    y_halves = []
    for hf in range(SSM_HALVES):
        l0 = hf * HALF_LANES
        y_halves.append(
            _dot_nt(h16_ref[:, l0:l0 + HALF_LANES], s5mat_ref[S5_C_RE, hf])
            + _dot_nt(h16_ref[:, PART_LANES + l0:PART_LANES + l0 + HALF_LANES],
                      s5mat_ref[S5_C_IM_NEG, hf]))
    s = jnp.concatenate(y_halves, axis=-1) + d_skip * ub_ref[...]

    gl = jax.nn.gelu(s).astype(BF16)
    branch_b = _dot(gl, w_glu_v_ref[...]) * _sigmoid(_dot(gl, w_glu_g_ref[...]))
    merged = (_sigmoid(gates_ref[:, 0:D_MODEL]) * ya_ref[...]
              + _sigmoid(gates_ref[:, D_MODEL:2 * D_MODEL]) * branch_b).astype(BF16)
    if permute:
        block_rows = n_seq * PERM_T
        blocks = [_dot(perm_t_ref[...], merged[j * block_rows:(j + 1) * block_rows]).astype(BF16)
                  for j in range(t_chunk // PERM_T)]
        merged = jnp.concatenate(
            [blk[b * PERM_T:(b + 1) * PERM_T] for b in range(n_seq) for blk in blocks], axis=0)
    x1 = x + _dot(merged, w_out_ref[...])
    x1_ref[...] = x1.reshape(x1_ref.shape)

    if cast_weights:
        @pl.when(step == pl.num_programs(0) - 1)
        def _finish_exports():
            for i in range(len(big_vmem)):
                export_copy(i).wait()


def _mixer_call(x, pool_in, h0, small_w, big_w, *, n_seq, t_chunk, pos_start):
    rows = n_seq * t_chunk
    permute = t_chunk > 1
    has_state = pool_in is not None
    cast_weights = big_w[0].dtype == F32
    if permute:
        n_steps = x.shape[1] // t_chunk
        x_spec = pl.BlockSpec((n_seq, t_chunk, D_MODEL), lambda i: (0, i, 0))
    else:
        n_steps = 1
        x_spec = pl.BlockSpec((1, n_seq, D_MODEL), lambda i: (0, 0, 0))
    keep_rows = POOL_BUF * n_seq

    in_arrays = [x]
    in_specs = [x_spec]
    if has_state:
        in_arrays += [pool_in, h0]
        in_specs += [_resident(pool_in.shape), _resident(h0.shape)]
    if permute:
        block_rows = n_seq * PERM_T
        r = np.arange(block_rows)
        perm = np.zeros((block_rows, block_rows), np.float32)
        perm[r, (r % n_seq) * PERM_T + r // n_seq] = 1.0
        in_arrays += [jnp.asarray(perm, BF16), jnp.asarray(perm.T, BF16)]
        in_specs += [_resident(perm.shape), _resident(perm.shape)]
    in_arrays += list(small_w) + list(big_w)
    in_specs += [_resident(w.shape) for w in small_w]
    in_specs += [_IN_HBM if cast_weights else _resident(w.shape) for w in big_w]

    out_shape = [
        jax.ShapeDtypeStruct(x.shape, F32),
        jax.ShapeDtypeStruct((keep_rows, POOL_WIDTH), F32),
        jax.ShapeDtypeStruct((n_seq, PART_LANES), F32),
        jax.ShapeDtypeStruct((n_seq, PART_LANES), F32),
    ]
    out_specs = [
        x_spec,
        pl.BlockSpec((keep_rows, POOL_WIDTH), lambda i: (0, 0)),
        pl.BlockSpec((n_seq, PART_LANES), lambda i: (0, 0)),
        pl.BlockSpec((n_seq, PART_LANES), lambda i: (0, 0)),
    ]
    scratch = [
        pltpu.VMEM((rows, 2 * D_MODEL), F32),
        pltpu.VMEM((rows, SSM_WIDTH), F32),
        pltpu.VMEM(((POOL_SLOTS + t_chunk) * n_seq, POOL_WIDTH), F32),
        pltpu.VMEM((rows, STATE_LANES), F32),
        pltpu.VMEM((rows, STATE_LANES), BF16),
        pltpu.VMEM((n_seq, STATE_LANES), F32),
        pltpu.VMEM((rows, D_MODEL), F32),
    ]
    if cast_weights:
        out_shape += [jax.ShapeDtypeStruct(w.shape, BF16) for w in big_w]
        out_specs += [_IN_HBM for _ in big_w]
        scratch += [pltpu.VMEM(w.shape, BF16) for w in big_w]
        scratch += [pltpu.VMEM((2, CAST_ROWS, CAST_COLS), F32),
                    pltpu.SemaphoreType.DMA((2,)),
                    pltpu.SemaphoreType.DMA((len(big_w),))]
    kern = functools.partial(_mixer_kernel, n_seq=n_seq, t_chunk=t_chunk, permute=permute,
                             pos_start=pos_start, has_state=has_state, cast_weights=cast_weights)
    return pl.pallas_call(
        kern,
        grid=(n_steps,),
        in_specs=in_specs,
        out_specs=out_specs,
        out_shape=out_shape,
        scratch_shapes=scratch,
        compiler_params=pltpu.CompilerParams(
            dimension_semantics=("arbitrary",), vmem_limit_bytes=VMEM_LIMIT_BYTES),
        name=f"mixer_n{n_seq}_t{t_chunk}",
    )(*in_arrays)


def _ffn_rows(x1, p, vec_ref, w_ff1_ref, w_ff2_ref, w_ple_ref, w_ple_gate_ref):
    h2 = _rmsnorm(x1, vec_ref[VEC_G_FF:VEC_G_FF + 1, :]).astype(BF16)
    x2 = x1
    for c in range(D_FF // FF_CHUNK):
        hid = _dot(h2, w_ff1_ref[:, c * FF_CHUNK:(c + 1) * FF_CHUNK])
        hid = jnp.square(jnp.maximum(hid, 0.0)).astype(BF16)
        x2 = x2 + _dot(hid, w_ff2_ref[c * FF_CHUNK:(c + 1) * FF_CHUNK, :])
    h3 = _rmsnorm(x2, vec_ref[VEC_G_PLE:VEC_G_PLE + 1, :]).astype(BF16)
    gate = _sigmoid(_dot(h3, w_ple_gate_ref[...]))
    x3 = x2 + _dot(p.astype(BF16), w_ple_ref[...]) * gate
    return _rmsnorm(x3, vec_ref[VEC_G_FINAL:VEC_G_FINAL + 1, :])


def _ffn_kernel(x1_ref, p_ref, x1s_ref, ps_ref, vec_ref, w_ff1_hbm, w_ff2_hbm, w_ple_hbm,
                w_ple_gate_hbm, out_ref, outs_ref, w_ff1_ref, w_ff2_ref, w_ple_ref,
                w_ple_gate_ref, stage_ref, cast_sem):
    step = pl.program_id(0)
    weights = (w_ff1_ref, w_ff2_ref, w_ple_ref, w_ple_gate_ref)

    @pl.when(step == 0)
    def _load_weights():
        _stream_cast(list(zip((w_ff1_hbm, w_ff2_hbm, w_ple_hbm, w_ple_gate_hbm), weights)),
                     stage_ref, cast_sem)

    out_ref[...] = _ffn_rows(x1_ref[...], p_ref[...], vec_ref, *weights)

    @pl.when(step == pl.num_programs(0) - 1)
    def _sample_rows():
        outs_ref[...] = _ffn_rows(x1s_ref[...], ps_ref[...], vec_ref, *weights)


def _ffn_call(x1, p, x1s, ps, vec, big_w, *, block_rows):
    n_rows = x1.shape[0]
    in_specs = [pl.BlockSpec((block_rows, D_MODEL), lambda i: (i, 0)),
                pl.BlockSpec((block_rows, PLE_DIM), lambda i: (i, 0)),
                _resident(x1s.shape), _resident(ps.shape), _resident(vec.shape)]
    in_specs += [_IN_HBM for _ in big_w]
    return pl.pallas_call(
        _ffn_kernel,
        grid=(n_rows // block_rows,),
        in_specs=in_specs,
        out_specs=(pl.BlockSpec((block_rows, D_MODEL), lambda i: (i, 0)),
                   pl.BlockSpec(x1s.shape, lambda i: (0, 0))),
        out_shape=(jax.ShapeDtypeStruct((n_rows, D_MODEL), F32),
                   jax.ShapeDtypeStruct(x1s.shape, F32)),
        scratch_shapes=[pltpu.VMEM(w.shape, BF16) for w in big_w] + [
            pltpu.VMEM((2, CAST_ROWS, CAST_COLS), F32), pltpu.SemaphoreType.DMA((2,))],
        compiler_params=pltpu.CompilerParams(
            dimension_semantics=("arbitrary",), vmem_limit_bytes=VMEM_LIMIT_BYTES),
        name="ffn",
    )(x1, p, x1s, ps, vec, *big_w)


def _s5_params(lam_re, lam_im, log_dt, b_re, b_im, c_re, c_im):
    dt = jnp.exp(log_dt)[:, None]
    mag = jnp.exp(lam_re * dt)
    ang = lam_im * dt
    abar_re = mag * jnp.cos(ang)
    abar_im = mag * jnp.sin(ang)
    den = lam_re * lam_re + lam_im * lam_im
    nr = abar_re - 1.0
    ni = abar_im
    k_re = ((nr * lam_re + ni * lam_im) / den)[:, None, :]
    k_im = ((ni * lam_re - nr * lam_im) / den)[:, None, :]
    b_re_t = jnp.swapaxes(b_re, 1, 2)
    b_im_t = jnp.swapaxes(b_im, 1, 2)
    bbar_re = k_re * b_re_t - k_im * b_im_t
    bbar_im = k_re * b_im_t + k_im * b_re_t
    abar = jnp.concatenate([abar_re.reshape(1, PART_LANES), abar_im.reshape(1, PART_LANES)],
                           axis=1)

    compact = jnp.stack([bbar_re, bbar_im, c_re, -c_im]).reshape(4 * SSM_WIDTH, SSM_STATE)
    src = lax.broadcasted_iota(jnp.int32, (SSM_STATE, HALF_LANES), 0)
    dst = lax.broadcasted_iota(jnp.int32, (SSM_STATE, HALF_LANES), 1)
    tiled = jnp.dot(compact, (src == dst % SSM_STATE).astype(F32))
    row_group = (lax.broadcasted_iota(jnp.int32, tiled.shape, 0) // SSM_GROUP_DIM) % HALF_GROUPS
    col_group = lax.broadcasted_iota(jnp.int32, tiled.shape, 1) // SSM_STATE
    s5mat = jnp.where(row_group == col_group, tiled, 0.0).astype(BF16).reshape(
        4, SSM_HALVES, HALF_IN, HALF_LANES)
    return abar, s5mat


def kernel(x_prompt, x_sample, p_prompt, p_sample, state_pool, state_ssm_re, state_ssm_im, g_mix, w_in, w_pool, pool_scale, lam_re, lam_im, log_dt, b_re, b_im, c_re, c_im, d_skip, w_glu_v, w_glu_g, w_out, g_ff, w_ff1, w_ff2, g_ple, w_ple, w_ple_gate, g_final):
    assert w_in.shape[0] == 1, "the final norm is fused into the (single) layer's ffn call"
    batch, seq_len, _ = x_prompt.shape
    dec_batch = x_sample.shape[0]

    abar, s5mat = _s5_params(lam_re[0], lam_im[0], log_dt[0], b_re[0], b_im[0], c_re[0], c_im[0])
    vec = jnp.stack([g_mix[0], pool_scale[0], jnp.concatenate([d_skip[0], d_skip[0]]),
                     g_ff[0], g_ple[0], g_final])
    small_w = (vec, w_pool[0].astype(BF16), abar, s5mat)

    x1p, pool_p, re_p, im_p, *mixer_w16 = _mixer_call(
        x_prompt, None, None, small_w, (w_in[0], w_glu_v[0], w_glu_g[0], w_out[0]),
        n_seq=batch, t_chunk=PROMPT_T_CHUNK, pos_start=0)
    pool_tm = jnp.swapaxes(state_pool[0], 0, 1).reshape(POOL_BUF * dec_batch, POOL_WIDTH)
    h0 = jnp.concatenate([state_ssm_re.reshape(dec_batch, PART_LANES),
                          state_ssm_im.reshape(dec_batch, PART_LANES)], axis=1)
    x1s, pool_s, re_s, im_s = _mixer_call(
        x_sample.reshape(1, dec_batch, D_MODEL), pool_tm, h0, small_w, mixer_w16,
        n_seq=dec_batch, t_chunk=1, pos_start=PAST_LEN)

    y_prompt, y_sample = _ffn_call(
        x1p.reshape(batch * seq_len, D_MODEL), p_prompt.reshape(batch * seq_len, PLE_DIM),
        x1s.reshape(dec_batch, D_MODEL), p_sample.reshape(dec_batch, PLE_DIM), vec,
        (w_ff1[0], w_ff2[0], w_ple[0], w_ple_gate[0]), block_rows=FFN_BLOCK_ROWS)

    def to_buf(tm, n):
        return jnp.swapaxes(tm.reshape(POOL_BUF, n, POOL_WIDTH), 0, 1)[None]

    state_shape = lambda n: (1, n, SSM_GROUPS, SSM_STATE)
    return (y_prompt.reshape(batch, seq_len, D_MODEL), y_sample.reshape(dec_batch, 1, D_MODEL),
            to_buf(pool_p, batch), to_buf(pool_s, dec_batch),
            re_p.reshape(state_shape(batch)), im_p.reshape(state_shape(batch)),
            re_s.reshape(state_shape(dec_batch)), im_s.reshape(state_shape(dec_batch)))
```

```python
import functools

import jax
import jax.numpy as jnp
import numpy as np
from jax import lax
from jax.experimental import pallas as pl
from jax.experimental.pallas import tpu as pltpu

D_MODEL = 1024
POOL_WINDOWS = (2, 4, 8, 16)
POOL_WIDTH = D_MODEL // 2
POOL_GROUP_DIM = POOL_WIDTH // len(POOL_WINDOWS)
POOL_OUT_GROUP_DIM = D_MODEL // len(POOL_WINDOWS)
POOL_BUF = max(POOL_WINDOWS) - 1
POOL_SLOTS = POOL_BUF + 1
SSM_WIDTH = D_MODEL // 2
SSM_GROUP_DIM = 16
SSM_GROUPS = SSM_WIDTH // SSM_GROUP_DIM
SSM_STATE = 64
PART_LANES = SSM_GROUPS * SSM_STATE
STATE_LANES = 2 * PART_LANES
SSM_HALVES = 2
HALF_IN = SSM_WIDTH // SSM_HALVES
HALF_GROUPS = SSM_GROUPS // SSM_HALVES
HALF_LANES = PART_LANES // SSM_HALVES
D_FF = 4 * D_MODEL
FF_CHUNK = 1024
PLE_DIM = 256
EPS = 1e-6
IN_WIDTH = POOL_WIDTH + SSM_WIDTH + 2 * D_MODEL
PAST_LEN = 16384
PROMPT_T_CHUNK = 64
PERM_T = 32
FFN_BLOCK_ROWS = 1024
SCAN_LANES = 512
SCAN_UNROLL = 32
BF16_TILE_ROWS = 16
CAST_ROWS, CAST_COLS = 512, 1024
VMEM_LIMIT_BYTES = 58 * 1024 * 1024

S5_B_RE, S5_B_IM, S5_C_RE, S5_C_IM_NEG = range(4)

BF16 = jnp.bfloat16
F32 = jnp.float32


def _rmsnorm(x, g):
    return x * lax.rsqrt(jnp.mean(x * x, axis=-1, keepdims=True) + EPS) * g


def _sigmoid(x):
    return 0.5 * jnp.tanh(0.5 * x) + 0.5


def _dot(a, b):
    return jnp.dot(a, b, preferred_element_type=F32)


def _dot_nt(a, b):
    return lax.dot_general(a, b, (((1,), (1,)), ((), ())), preferred_element_type=F32)


def _resident(shape):
    nd = len(shape)
    return pl.BlockSpec(shape, lambda i, _nd=nd: (0,) * _nd, pipeline_mode=pl.Buffered(1))


_IN_HBM = pl.BlockSpec(memory_space=pl.ANY)


def _cast_jobs(pairs):
    jobs = []
    for src, dst in pairs:
        n_rows, n_cols = src.shape
        rows = min(CAST_ROWS, n_rows)
        jobs += [(src, dst, r0, rows, c0)
                 for r0 in range(0, n_rows, rows) for c0 in range(0, n_cols, CAST_COLS)]
    return jobs


def _stream_cast(pairs, stage_ref, sem):
    jobs = _cast_jobs(pairs)

    def copy(k):
        src, _, r0, rows, c0 = jobs[k]
        return pltpu.make_async_copy(src.at[r0:r0 + rows, c0:c0 + CAST_COLS],
                                     stage_ref.at[k % 2, 0:rows], sem.at[k % 2])

    copy(0).start()
    for k, (_, dst, r0, rows, c0) in enumerate(jobs):
        if k + 1 < len(jobs):
            copy(k + 1).start()
        copy(k).wait()
        dst[r0:r0 + rows, c0:c0 + CAST_COLS] = stage_ref[k % 2, 0:rows].astype(BF16)


def _mixer_kernel(*refs, n_seq, t_chunk, permute, pos_start, has_state, cast_weights, n_later):
    it = iter(refs)
    x_ref = next(it)
    pool_in_ref = next(it) if has_state else None
    h0_ref = next(it) if has_state else None
    perm_ref = next(it) if permute else None
    perm_t_ref = next(it) if permute else None
    g_mix_ref, pool_scale_ref, d_skip_ref, w_pool_ref, abar_ref, s5mat_ref = (
        next(it) for _ in range(6))
    big_in = [next(it) for _ in range(4)]
    later_in = [next(it) for _ in range(n_later)]
    x1_ref, ua_out_ref, h_re_out_ref, h_im_out_ref = (next(it) for _ in range(4))
    big_out = [next(it) for _ in range(4)] if cast_weights else None
    later_out = [next(it) for _ in range(n_later)]
    gates_ref, ub_ref, ext_ref, hs_ref, h16_ref, hc_ref, ya_ref = (next(it) for _ in range(7))
    if cast_weights:
        big_vmem = [next(it) for _ in range(4)]
        stage_ref, stage16_ref, cast_sem, export_sem, send_sem = (next(it) for _ in range(5))
    else:
        big_vmem = big_in
    w_in_ref, w_glu_v_ref, w_glu_g_ref, w_out_ref = big_vmem

    rows = n_seq * t_chunk
    step = pl.program_id(0)
    cur0 = POOL_SLOTS * n_seq
    g_mix = g_mix_ref[...]
    pool_scale = pool_scale_ref[...]
    d_skip = d_skip_ref[...]

    def export_copy(i):
        return pltpu.make_async_copy(big_vmem[i], big_out[i], export_sem.at[i])

    later_jobs = _cast_jobs(list(zip(later_in, later_out)))

    def fetch(k):
        src, _, r0, n, c0 = later_jobs[k]
        return pltpu.make_async_copy(src.at[r0:r0 + n, c0:c0 + CAST_COLS],
                                     stage_ref.at[k % 2, 0:n], cast_sem.at[k % 2])

    def send(k):
        _, dst, r0, n, c0 = later_jobs[k]
        return pltpu.make_async_copy(stage16_ref.at[k % 2, 0:n],
                                     dst.at[r0:r0 + n, c0:c0 + CAST_COLS], send_sem.at[k % 2])

    @pl.when(step == 0)
    def _init():
        if has_state:
            ext_ref[n_seq:cur0, :] = pool_in_ref[...]
            hc_ref[...] = h0_ref[...]
        else:
            ext_ref[0:cur0, :] = jnp.zeros((cur0, POOL_WIDTH), F32)
            hc_ref[...] = jnp.zeros((n_seq, STATE_LANES), F32)
        if cast_weights:
            _stream_cast(list(zip(big_in, big_vmem)), stage_ref, cast_sem)
            for i in range(len(big_vmem)):
                export_copy(i).start()
            if later_jobs:
                fetch(0).start()

    def normed_time_major(src_ref):
        hn = _rmsnorm(src_ref[...].reshape(rows, D_MODEL), g_mix).astype(BF16)
        if not permute:
            return hn
        blocks = []
        for j in range(t_chunk // PERM_T):
            picked = jnp.concatenate(
                [hn[b * t_chunk + j * PERM_T:b * t_chunk + (j + 1) * PERM_T] for b in range(n_seq)],
                axis=0)
            blocks.append(_dot(perm_ref[...], picked).astype(BF16))
        return jnp.concatenate(blocks, axis=0)

    x = x_ref[...].reshape(rows, D_MODEL)
    h = normed_time_major(x_ref)
    g0 = POOL_WIDTH + SSM_WIDTH
    u = _dot(h, w_in_ref[:, 0:g0])
    ext_ref[cur0:cur0 + rows, :] = u[:, 0:POOL_WIDTH]
    ub_ref[...] = u[:, POOL_WIDTH:g0]
    u_b16 = u[:, POOL_WIDTH:g0].astype(BF16)
    for hf in range(SSM_HALVES):
        u_half = u_b16[:, hf * HALF_IN:(hf + 1) * HALF_IN]
        l0 = hf * HALF_LANES
        hs_ref[:, l0:l0 + HALF_LANES] = _dot(u_half, s5mat_ref[S5_B_RE, hf])
        hs_ref[:, PART_LANES + l0:PART_LANES + l0 + HALF_LANES] = _dot(
            u_half, s5mat_ref[S5_B_IM, hf])
    gates_ref[...] = _dot(h, w_in_ref[:, g0:IN_WIDTH])

    t_idx = lax.shift_right_logical(
        lax.broadcasted_iota(jnp.int32, (rows, POOL_GROUP_DIM), 0), n_seq.bit_length() - 1)
    pos = t_idx + (pos_start + step * t_chunk)
    for gi, w in enumerate(POOL_WINDOWS):
        c0 = gi * POOL_GROUP_DIM
        cur = ext_ref[cur0:cur0 + rows, c0:c0 + POOL_GROUP_DIM]
        acc = cur
        for k in range(1, w):
            r0 = cur0 - k * n_seq
            acc = acc + ext_ref[r0:r0 + rows, c0:c0 + POOL_GROUP_DIM]
        count = jnp.minimum(pos + 1, w).astype(F32)
        pooled = (acc / count - cur).astype(BF16)
        o0 = gi * POOL_OUT_GROUP_DIM
        ya_ref[:, o0:o0 + POOL_OUT_GROUP_DIM] = (
            _dot(pooled, w_pool_ref[gi].astype(BF16))
            * pool_scale[:, o0:o0 + POOL_OUT_GROUP_DIM])
    ua_out_ref[...] = ext_ref[cur0 + rows - POOL_BUF * n_seq:cur0 + rows, :]
    if t_chunk >= POOL_SLOTS:
        ext_ref[0:cur0, :] = ext_ref[rows:rows + cur0, :]

    steps_per_store = max(1, BF16_TILE_ROWS // n_seq)
    store_rows = steps_per_store * n_seq
    for q in range(PART_LANES // SCAN_LANES):
        re0 = q * SCAN_LANES
        im0 = PART_LANES + re0
        a_re = abar_ref[:, re0:re0 + SCAN_LANES]
        a_im = abar_ref[:, im0:im0 + SCAN_LANES]
        if t_chunk > 1:
            a_re = jnp.broadcast_to(a_re, (n_seq, SCAN_LANES))
            a_im = jnp.broadcast_to(a_im, (n_seq, SCAN_LANES))

        def scan_steps(i, carry, re0=re0, im0=im0, a_re=a_re, a_im=a_im):
            h_re, h_im = carry
            r0 = i * store_rows if isinstance(i, int) else pl.multiple_of(i * store_rows,
                                                                          store_rows)
            new_re, new_im = [], []
            for s in range(steps_per_store):
                rs = r0 + s * n_seq
                h_re, h_im = (
                    a_re * h_re - a_im * h_im + hs_ref[pl.ds(rs, n_seq), re0:re0 + SCAN_LANES],
                    a_re * h_im + a_im * h_re + hs_ref[pl.ds(rs, n_seq), im0:im0 + SCAN_LANES])
                new_re.append(h_re)
                new_im.append(h_im)
            h16_ref[pl.ds(r0, store_rows), re0:re0 + SCAN_LANES] = (
                jnp.concatenate(new_re, axis=0).astype(BF16))
            h16_ref[pl.ds(r0, store_rows), im0:im0 + SCAN_LANES] = (
                jnp.concatenate(new_im, axis=0).astype(BF16))
            return h_re, h_im

        carry = (hc_ref[:, re0:re0 + SCAN_LANES], hc_ref[:, im0:im0 + SCAN_LANES])
        if t_chunk == steps_per_store:
            carry = scan_steps(0, carry)
        else:
            carry = lax.fori_loop(0, t_chunk // steps_per_store, scan_steps, carry,
                                  unroll=SCAN_UNROLL)
        hc_ref[:, re0:re0 + SCAN_LANES] = carry[0]
        hc_ref[:, im0:im0 + SCAN_LANES] = carry[1]
    h_re_out_ref[...] = hc_ref[:, 0:PART_LANES]
    h_im_out_ref[...] = hc_ref[:, PART_LANES:STATE_LANES]

    y_halves = []
    for hf in range(SSM_HALVES):
        l0 = hf * HALF_LANES
        y_halves.append(
            _dot_nt(h16_ref[:, l0:l0 + HALF_LANES], s5mat_ref[S5_C_RE, hf])
            + _dot_nt(h16_ref[:, PART_LANES + l0:PART_LANES + l0 + HALF_LANES],
                      s5mat_ref[S5_C_IM_NEG, hf]))
    s = jnp.concatenate(y_halves, axis=-1) + d_skip * ub_ref[...]

    gl = jax.nn.gelu(s).astype(BF16)
    branch_b = _dot(gl, w_glu_v_ref[...]) * _sigmoid(_dot(gl, w_glu_g_ref[...]))
    merged = (_sigmoid(gates_ref[:, 0:D_MODEL]) * ya_ref[...]
              + _sigmoid(gates_ref[:, D_MODEL:2 * D_MODEL]) * branch_b).astype(BF16)
    if permute:
        block_rows = n_seq * PERM_T
        blocks = [_dot(perm_t_ref[...], merged[j * block_rows:(j + 1) * block_rows]).astype(BF16)
                  for j in range(t_chunk // PERM_T)]
        merged = jnp.concatenate(
            [blk[b * PERM_T:(b + 1) * PERM_T] for b in range(n_seq) for blk in blocks], axis=0)
    x1 = x + _dot(merged, w_out_ref[...])
    x1_ref[...] = x1.reshape(x1_ref.shape)

    if cast_weights:
        @pl.when(step == pl.num_programs(0) - 1)
        def _finish_exports():
            for i in range(len(big_vmem)):
                export_copy(i).wait()

    for k, (_, _, _, n, _) in enumerate(later_jobs):
        @pl.when(step == k + 1)
        def _convert_chunk(k=k, n=n):
            if k >= 2:
                send(k - 2).wait()
            fetch(k).wait()
            stage16_ref[k % 2, 0:n] = stage_ref[k % 2, 0:n].astype(BF16)
            send(k).start()
            if k + 1 < len(later_jobs):
                fetch(k + 1).start()
    if later_jobs:
        @pl.when(step == len(later_jobs) + 1)
        def _finish_sends():
            for k in range(max(0, len(later_jobs) - 2), len(later_jobs)):
                send(k).wait()


def _mixer_call(x, pool_in, h0, small_w, big_w, later_w=(), *, n_seq, t_chunk, pos_start):
    rows = n_seq * t_chunk
    permute = t_chunk > 1
    has_state = pool_in is not None
    cast_weights = big_w[0].dtype == F32
    if permute:
        n_steps = x.shape[1] // t_chunk
        x_spec = pl.BlockSpec((n_seq, t_chunk, D_MODEL), lambda i: (0, i, 0))
    else:
        n_steps = 1
        x_spec = pl.BlockSpec((1, n_seq, D_MODEL), lambda i: (0, 0, 0))
    keep_rows = POOL_BUF * n_seq

    in_arrays = [x]
    in_specs = [x_spec]
    if has_state:
        in_arrays += [pool_in, h0]
        in_specs += [_resident(pool_in.shape), _resident(h0.shape)]
    if permute:
        block_rows = n_seq * PERM_T
        r = np.arange(block_rows)
        perm = np.zeros((block_rows, block_rows), np.float32)
        perm[r, (r % n_seq) * PERM_T + r // n_seq] = 1.0
        in_arrays += [jnp.asarray(perm, BF16), jnp.asarray(perm.T, BF16)]
        in_specs += [_resident(perm.shape), _resident(perm.shape)]
    assert cast_weights or not later_w
    n_chunks = sum((w.shape[0] // min(CAST_ROWS, w.shape[0])) * (w.shape[1] // CAST_COLS)
                   for w in later_w)
    assert n_chunks + 2 <= n_steps or not later_w, "one chunk of later_w per grid step"
    in_arrays += list(small_w) + list(big_w) + list(later_w)
    in_specs += [_resident(w.shape) for w in small_w]
    in_specs += [_IN_HBM if cast_weights else _resident(w.shape) for w in big_w]
    in_specs += [_IN_HBM for _ in later_w]

    out_shape = [
        jax.ShapeDtypeStruct(x.shape, F32),
        jax.ShapeDtypeStruct((keep_rows, POOL_WIDTH), F32),
        jax.ShapeDtypeStruct((n_seq, PART_LANES), F32),
        jax.ShapeDtypeStruct((n_seq, PART_LANES), F32),
    ]
    out_specs = [
        x_spec,
        pl.BlockSpec((keep_rows, POOL_WIDTH), lambda i: (0, 0)),
        pl.BlockSpec((n_seq, PART_LANES), lambda i: (0, 0)),
        pl.BlockSpec((n_seq, PART_LANES), lambda i: (0, 0)),
    ]
    scratch = [
        pltpu.VMEM((rows, 2 * D_MODEL), F32),
        pltpu.VMEM((rows, SSM_WIDTH), F32),
        pltpu.VMEM(((POOL_SLOTS + t_chunk) * n_seq, POOL_WIDTH), F32),
        pltpu.VMEM((rows, STATE_LANES), F32),
        pltpu.VMEM((rows, STATE_LANES), BF16),
        pltpu.VMEM((n_seq, STATE_LANES), F32),
        pltpu.VMEM((rows, D_MODEL), F32),
    ]
    if cast_weights:
        out_shape += [jax.ShapeDtypeStruct(w.shape, BF16) for w in list(big_w) + list(later_w)]
        out_specs += [_IN_HBM for _ in list(big_w) + list(later_w)]
        scratch += [pltpu.VMEM(w.shape, BF16) for w in big_w]
        scratch += [pltpu.VMEM((2, CAST_ROWS, CAST_COLS), F32),
                    pltpu.VMEM((2, CAST_ROWS, CAST_COLS), BF16),
                    pltpu.SemaphoreType.DMA((2,)),
                    pltpu.SemaphoreType.DMA((len(big_w),)),
                    pltpu.SemaphoreType.DMA((2,))]
    kern = functools.partial(_mixer_kernel, n_seq=n_seq, t_chunk=t_chunk, permute=permute,
                             pos_start=pos_start, has_state=has_state, cast_weights=cast_weights,
                             n_later=len(later_w))
    return pl.pallas_call(
        kern,
        grid=(n_steps,),
        in_specs=in_specs,
        out_specs=out_specs,
        out_shape=out_shape,
        scratch_shapes=scratch,
        compiler_params=pltpu.CompilerParams(
            dimension_semantics=("arbitrary",), vmem_limit_bytes=VMEM_LIMIT_BYTES),
        name=f"mixer_n{n_seq}_t{t_chunk}",
    )(*in_arrays)


def _ffn_rows(x1, p, g_ff_ref, g_ple_ref, g_final_ref, w_ff1_ref, w_ff2_ref, w_ple_ref,
              w_ple_gate_ref):
    h2 = _rmsnorm(x1, g_ff_ref[...]).astype(BF16)
    x2 = x1
    for c in range(D_FF // FF_CHUNK):
        hid = _dot(h2, w_ff1_ref[:, c * FF_CHUNK:(c + 1) * FF_CHUNK])
        hid = jnp.square(jnp.maximum(hid, 0.0)).astype(BF16)
        x2 = x2 + _dot(hid, w_ff2_ref[c * FF_CHUNK:(c + 1) * FF_CHUNK, :])
    h3 = _rmsnorm(x2, g_ple_ref[...]).astype(BF16)
    gate = _sigmoid(_dot(h3, w_ple_gate_ref[...]))
    x3 = x2 + _dot(p.astype(BF16), w_ple_ref[...]) * gate
    return _rmsnorm(x3, g_final_ref[...])


def _ffn_kernel(x1_ref, p_ref, x1s_ref, ps_ref, *refs):
    params, (out_ref, outs_ref) = refs[:-2], refs[-2:]
    out_ref[...] = _ffn_rows(x1_ref[...], p_ref[...], *params)

    @pl.when(pl.program_id(0) == pl.num_programs(0) - 1)
    def _sample_rows():
        outs_ref[...] = _ffn_rows(x1s_ref[...], ps_ref[...], *params)


def _ffn_call(x1, p, x1s, ps, weights, *, block_rows):
    n_rows = x1.shape[0]
    in_specs = [pl.BlockSpec((block_rows, D_MODEL), lambda i: (i, 0)),
                pl.BlockSpec((block_rows, PLE_DIM), lambda i: (i, 0)),
                _resident(x1s.shape), _resident(ps.shape)]
    in_specs += [_resident(w.shape) for w in weights]
    return pl.pallas_call(
        _ffn_kernel,
        grid=(n_rows // block_rows,),
        in_specs=in_specs,
        out_specs=(pl.BlockSpec((block_rows, D_MODEL), lambda i: (i, 0)),
                   pl.BlockSpec(x1s.shape, lambda i: (0, 0))),
        out_shape=(jax.ShapeDtypeStruct((n_rows, D_MODEL), F32),
                   jax.ShapeDtypeStruct(x1s.shape, F32)),
        compiler_params=pltpu.CompilerParams(
            dimension_semantics=("arbitrary",), vmem_limit_bytes=VMEM_LIMIT_BYTES),
        name="ffn",
    )(x1, p, x1s, ps, *weights)


def _s5_params(lam_re, lam_im, log_dt, b_re, b_im, c_re, c_im):
    dt = jnp.exp(log_dt)[:, None]
    mag = jnp.exp(lam_re * dt)
    ang = lam_im * dt
    abar_re = mag * jnp.cos(ang)
    abar_im = mag * jnp.sin(ang)
    den = lam_re * lam_re + lam_im * lam_im
    nr = abar_re - 1.0
    ni = abar_im
    k_re = ((nr * lam_re + ni * lam_im) / den)[:, None, :]
    k_im = ((ni * lam_re - nr * lam_im) / den)[:, None, :]
    b_re_t = jnp.swapaxes(b_re, 1, 2)
    b_im_t = jnp.swapaxes(b_im, 1, 2)
    bbar_re = k_re * b_re_t - k_im * b_im_t
    bbar_im = k_re * b_im_t + k_im * b_re_t
    abar = jnp.concatenate([abar_re.reshape(1, PART_LANES), abar_im.reshape(1, PART_LANES)],
                           axis=1)

    compact = jnp.stack([bbar_re, bbar_im, c_re, c_im]).reshape(4 * SSM_WIDTH, SSM_STATE)
    src = lax.broadcasted_iota(jnp.int32, (SSM_STATE, HALF_LANES), 0)
    dst = lax.broadcasted_iota(jnp.int32, (SSM_STATE, HALF_LANES), 1)
    tiled = jnp.dot(compact, (src == dst % SSM_STATE).astype(F32))
    row = lax.broadcasted_iota(jnp.int32, tiled.shape, 0)
    col_group = lax.broadcasted_iota(jnp.int32, tiled.shape, 1) // SSM_STATE
    keep = (row // SSM_GROUP_DIM) % HALF_GROUPS == col_group
    signed = jnp.where(row >= S5_C_IM_NEG * SSM_WIDTH, -tiled, tiled)
    s5mat = jnp.where(keep, signed, 0.0).astype(BF16).reshape(4, SSM_HALVES, HALF_IN, HALF_LANES)
    return abar, s5mat


def kernel(x_prompt, x_sample, p_prompt, p_sample, state_pool, state_ssm_re, state_ssm_im, g_mix, w_in, w_pool, pool_scale, lam_re, lam_im, log_dt, b_re, b_im, c_re, c_im, d_skip, w_glu_v, w_glu_g, w_out, g_ff, w_ff1, w_ff2, g_ple, w_ple, w_ple_gate, g_final):
    assert w_in.shape[0] == 1, "the final norm is fused into the (single) layer's ffn call"
    batch, seq_len, _ = x_prompt.shape
    dec_batch = x_sample.shape[0]

    abar, s5mat = _s5_params(lam_re[0], lam_im[0], log_dt[0], b_re[0], b_im[0], c_re[0], c_im[0])
    small_w = (g_mix, pool_scale, d_skip, w_pool[0], abar, s5mat)

    x1p, pool_p, re_p, im_p, *w16 = _mixer_call(
        x_prompt, None, None, small_w, (w_in[0], w_glu_v[0], w_glu_g[0], w_out[0]),
        (w_ff1[0], w_ff2[0], w_ple[0], w_ple_gate[0]),
        n_seq=batch, t_chunk=PROMPT_T_CHUNK, pos_start=0)
    mixer_w16, ffn_w16 = w16[:4], w16[4:]
    pool_tm = jnp.swapaxes(state_pool[0], 0, 1).reshape(POOL_BUF * dec_batch, POOL_WIDTH)
    h0 = jnp.concatenate([state_ssm_re.reshape(dec_batch, PART_LANES),
                          state_ssm_im.reshape(dec_batch, PART_LANES)], axis=1)
    x1s, pool_s, re_s, im_s = _mixer_call(
        x_sample.reshape(1, dec_batch, D_MODEL), pool_tm, h0, small_w, mixer_w16,
        n_seq=dec_batch, t_chunk=1, pos_start=PAST_LEN)

    y_prompt, y_sample = _ffn_call(
        x1p.reshape(batch * seq_len, D_MODEL), p_prompt.reshape(batch * seq_len, PLE_DIM),
        x1s.reshape(dec_batch, D_MODEL), p_sample.reshape(dec_batch, PLE_DIM),
        [g_ff, g_ple, g_final.reshape(1, D_MODEL)] + ffn_w16, block_rows=FFN_BLOCK_ROWS)

    def to_buf(tm, n):
        return jnp.swapaxes(tm.reshape(POOL_BUF, n, POOL_WIDTH), 0, 1)[None]

    state_shape = lambda n: (1, n, SSM_GROUPS, SSM_STATE)
    return (y_prompt.reshape(batch, seq_len, D_MODEL), y_sample.reshape(dec_batch, 1, D_MODEL),
            to_buf(pool_p, batch), to_buf(pool_s, dec_batch),
            re_p.reshape(state_shape(batch)), im_p.reshape(state_shape(batch)),
            re_s.reshape(state_shape(dec_batch)), im_s.reshape(state_shape(dec_batch)))
```

```python
import functools

import jax
import jax.numpy as jnp
from jax import lax
from jax.experimental import pallas as pl
from jax.experimental.pallas import tpu as pltpu

D_MODEL = 1024
POOL_WINDOWS = (2, 4, 8, 16)
POOL_WIDTH = D_MODEL // 2
POOL_GROUP_DIM = POOL_WIDTH // len(POOL_WINDOWS)
POOL_OUT_GROUP_DIM = D_MODEL // len(POOL_WINDOWS)
POOL_BUF = max(POOL_WINDOWS) - 1
POOL_SLOTS = POOL_BUF + 1
SSM_WIDTH = D_MODEL // 2
SSM_GROUP_DIM = 16
SSM_GROUPS = SSM_WIDTH // SSM_GROUP_DIM
SSM_STATE = 64
PART_LANES = SSM_GROUPS * SSM_STATE
STATE_LANES = 2 * PART_LANES
SSM_HALVES = 2
HALF_IN = SSM_WIDTH // SSM_HALVES
HALF_GROUPS = SSM_GROUPS // SSM_HALVES
HALF_LANES = PART_LANES // SSM_HALVES
D_FF = 4 * D_MODEL
FF_CHUNK = 1024
PLE_DIM = 256
EPS = 1e-6
IN_WIDTH = POOL_WIDTH + SSM_WIDTH + 2 * D_MODEL
PAST_LEN = 16384
PROMPT_T_CHUNK = 64
FFN_BLOCK_ROWS = 1024
SCAN_LANES = 512
SCAN_UNROLL = 32
BF16_TILE_ROWS = 16
CAST_ROWS, CAST_COLS = 512, 1024
VMEM_LIMIT_BYTES = 58 * 1024 * 1024

S5_B_RE, S5_B_IM, S5_C_RE, S5_C_IM_NEG = range(4)

BF16 = jnp.bfloat16
F32 = jnp.float32


def _rmsnorm(x, g):
    return x * lax.rsqrt(jnp.mean(x * x, axis=-1, keepdims=True) + EPS) * g


def _sigmoid(x):
    return 0.5 * jnp.tanh(0.5 * x) + 0.5


def _dot(a, b):
    return jnp.dot(a, b, preferred_element_type=F32)


def _dot_nt(a, b):
    return lax.dot_general(a, b, (((1,), (1,)), ((), ())), preferred_element_type=F32)


def _resident(shape):
    nd = len(shape)
    return pl.BlockSpec(shape, lambda i, _nd=nd: (0,) * _nd, pipeline_mode=pl.Buffered(1))


_IN_HBM = pl.BlockSpec(memory_space=pl.ANY)


def _cast_jobs(pairs):
    jobs = []
    for src, dst in pairs:
        n_rows, n_cols = src.shape
        rows = min(CAST_ROWS, n_rows)
        jobs += [(src, dst, r0, rows, c0)
                 for r0 in range(0, n_rows, rows) for c0 in range(0, n_cols, CAST_COLS)]
    return jobs


def _stream_cast(pairs, stage_ref, sem):
    jobs = _cast_jobs(pairs)

    def copy(k):
        src, _, r0, rows, c0 = jobs[k]
        return pltpu.make_async_copy(src.at[r0:r0 + rows, c0:c0 + CAST_COLS],
                                     stage_ref.at[k % 2, 0:rows], sem.at[k % 2])

    copy(0).start()
    for k, (_, dst, r0, rows, c0) in enumerate(jobs):
        if k + 1 < len(jobs):
            copy(k + 1).start()
        copy(k).wait()
        dst[r0:r0 + rows, c0:c0 + CAST_COLS] = stage_ref[k % 2, 0:rows].astype(BF16)


def _mixer_kernel(*refs, n_seq, t_chunk, stream_io, pos_start, has_state, cast_weights, n_later):
    it = iter(refs)
    x_ref = next(it)
    pool_in_ref = next(it) if has_state else None
    h0_ref = next(it) if has_state else None
    g_mix_ref, pool_scale_ref, d_skip_ref, w_pool_ref, abar_ref, s5mat_ref = (
        next(it) for _ in range(6))
    big_in = [next(it) for _ in range(4)]
    later_in = [next(it) for _ in range(n_later)]
    x1_ref, ua_out_ref, h_re_out_ref, h_im_out_ref = (next(it) for _ in range(4))
    big_out = [next(it) for _ in range(4)] if cast_weights else None
    later_out = [next(it) for _ in range(n_later)]
    gates_ref, ub_ref, ext_ref, hs_ref, h16_ref, hc_ref, ya_ref = (next(it) for _ in range(7))
    if cast_weights:
        big_vmem = [next(it) for _ in range(4)]
        stage_ref, stage16_ref, cast_sem, export_sem, send_sem = (next(it) for _ in range(5))
    else:
        big_vmem = big_in
    if stream_io:
        xt_ref, x1t_ref, in_sem, out_sem = (next(it) for _ in range(4))
    w_in_ref, w_glu_v_ref, w_glu_g_ref, w_out_ref = big_vmem

    rows = n_seq * t_chunk
    step = pl.program_id(0)
    last_step = pl.num_programs(0) - 1

    def load_x(i, slot):
        return [pltpu.make_async_copy(x_ref.at[b, pl.ds(i * t_chunk, t_chunk), :],
                                      xt_ref.at[slot, :, b, :], in_sem.at[slot, b])
                for b in range(n_seq)]

    def store_x1(i, slot):
        return [pltpu.make_async_copy(x1t_ref.at[slot, :, b, :],
                                      x1_ref.at[b, pl.ds(i * t_chunk, t_chunk), :],
                                      out_sem.at[slot, b])
                for b in range(n_seq)]
    cur0 = POOL_SLOTS * n_seq
    g_mix = g_mix_ref[...]
    pool_scale = pool_scale_ref[...]
    d_skip = d_skip_ref[...]

    def export_copy(i):
        return pltpu.make_async_copy(big_vmem[i], big_out[i], export_sem.at[i])

    later_jobs = _cast_jobs(list(zip(later_in, later_out)))

    def fetch(k):
        src, _, r0, n, c0 = later_jobs[k]
        return pltpu.make_async_copy(src.at[r0:r0 + n, c0:c0 + CAST_COLS],
                                     stage_ref.at[k % 2, 0:n], cast_sem.at[k % 2])

    def send(k):
        _, dst, r0, n, c0 = later_jobs[k]
        return pltpu.make_async_copy(stage16_ref.at[k % 2, 0:n],
                                     dst.at[r0:r0 + n, c0:c0 + CAST_COLS], send_sem.at[k % 2])

    @pl.when(step == 0)
    def _init():
        if stream_io:
            for c in load_x(0, 0):
                c.start()
        if has_state:
            ext_ref[n_seq:cur0, :] = pool_in_ref[...]
            hc_ref[...] = h0_ref[...]
        else:
            ext_ref[0:cur0, :] = jnp.zeros((cur0, POOL_WIDTH), F32)
            hc_ref[...] = jnp.zeros((n_seq, STATE_LANES), F32)
        if cast_weights:
            _stream_cast(list(zip(big_in, big_vmem)), stage_ref, cast_sem)
            for i in range(len(big_vmem)):
                export_copy(i).start()
            if later_jobs:
                fetch(0).start()

    if stream_io:
        slot = lax.rem(step, 2)

        @pl.when(step < last_step)
        def _prefetch():
            for c in load_x(step + 1, 1 - slot):
                c.start()

        for c in load_x(step, slot):
            c.wait()

        @pl.when(step >= 2)
        def _free_out_slot():
            for c in store_x1(step - 2, slot):
                c.wait()

        x = xt_ref[slot].reshape(rows, D_MODEL)
    else:
        x = x_ref[...].reshape(rows, D_MODEL)
    h = _rmsnorm(x, g_mix).astype(BF16)
    g0 = POOL_WIDTH + SSM_WIDTH
    u = _dot(h, w_in_ref[:, 0:g0])
    ext_ref[cur0:cur0 + rows, :] = u[:, 0:POOL_WIDTH]
    ub_ref[...] = u[:, POOL_WIDTH:g0]
    u_b16 = u[:, POOL_WIDTH:g0].astype(BF16)
    for hf in range(SSM_HALVES):
        u_half = u_b16[:, hf * HALF_IN:(hf + 1) * HALF_IN]
        l0 = hf * HALF_LANES
        hs_ref[:, l0:l0 + HALF_LANES] = _dot(u_half, s5mat_ref[S5_B_RE, hf])
        hs_ref[:, PART_LANES + l0:PART_LANES + l0 + HALF_LANES] = _dot(
            u_half, s5mat_ref[S5_B_IM, hf])
    gates_ref[...] = _dot(h, w_in_ref[:, g0:IN_WIDTH])

    t_idx = lax.shift_right_logical(
        lax.broadcasted_iota(jnp.int32, (rows, POOL_GROUP_DIM), 0), n_seq.bit_length() - 1)
    pos = t_idx + (pos_start + step * t_chunk)
    for gi, w in enumerate(POOL_WINDOWS):
        c0 = gi * POOL_GROUP_DIM
        cur = ext_ref[cur0:cur0 + rows, c0:c0 + POOL_GROUP_DIM]
        acc = cur
        for k in range(1, w):
            r0 = cur0 - k * n_seq
            acc = acc + ext_ref[r0:r0 + rows, c0:c0 + POOL_GROUP_DIM]
        count = jnp.minimum(pos + 1, w).astype(F32)
        pooled = (acc / count - cur).astype(BF16)
        o0 = gi * POOL_OUT_GROUP_DIM
        ya_ref[:, o0:o0 + POOL_OUT_GROUP_DIM] = (
            _dot(pooled, w_pool_ref[gi].astype(BF16))
            * pool_scale[:, o0:o0 + POOL_OUT_GROUP_DIM])
    ua_out_ref[...] = ext_ref[cur0 + rows - POOL_BUF * n_seq:cur0 + rows, :]
    if t_chunk >= POOL_SLOTS:
        ext_ref[0:cur0, :] = ext_ref[rows:rows + cur0, :]

    steps_per_store = max(1, BF16_TILE_ROWS // n_seq)
    store_rows = steps_per_store * n_seq
    for q in range(PART_LANES // SCAN_LANES):
        re0 = q * SCAN_LANES
        im0 = PART_LANES + re0
        a_re = abar_ref[:, re0:re0 + SCAN_LANES]
        a_im = abar_ref[:, im0:im0 + SCAN_LANES]
        if t_chunk > 1:
            a_re = jnp.broadcast_to(a_re, (n_seq, SCAN_LANES))
            a_im = jnp.broadcast_to(a_im, (n_seq, SCAN_LANES))

        def scan_steps(i, carry, re0=re0, im0=im0, a_re=a_re, a_im=a_im):
            h_re, h_im = carry
            r0 = i * store_rows if isinstance(i, int) else pl.multiple_of(i * store_rows,
                                                                          store_rows)
            new_re, new_im = [], []
            for s in range(steps_per_store):
                rs = r0 + s * n_seq
                h_re, h_im = (
                    a_re * h_re - a_im * h_im + hs_ref[pl.ds(rs, n_seq), re0:re0 + SCAN_LANES],
                    a_re * h_im + a_im * h_re + hs_ref[pl.ds(rs, n_seq), im0:im0 + SCAN_LANES])
                new_re.append(h_re)
                new_im.append(h_im)
            h16_ref[pl.ds(r0, store_rows), re0:re0 + SCAN_LANES] = (
                jnp.concatenate(new_re, axis=0).astype(BF16))
            h16_ref[pl.ds(r0, store_rows), im0:im0 + SCAN_LANES] = (
                jnp.concatenate(new_im, axis=0).astype(BF16))
            return h_re, h_im

        carry = (hc_ref[:, re0:re0 + SCAN_LANES], hc_ref[:, im0:im0 + SCAN_LANES])
        if t_chunk == steps_per_store:
            carry = scan_steps(0, carry)
        else:
            carry = lax.fori_loop(0, t_chunk // steps_per_store, scan_steps, carry,
                                  unroll=SCAN_UNROLL)
        hc_ref[:, re0:re0 + SCAN_LANES] = carry[0]
        hc_ref[:, im0:im0 + SCAN_LANES] = carry[1]
    h_re_out_ref[...] = hc_ref[:, 0:PART_LANES]
    h_im_out_ref[...] = hc_ref[:, PART_LANES:STATE_LANES]

    y_halves = []
    for hf in range(SSM_HALVES):
        l0 = hf * HALF_LANES
        y_halves.append(
            _dot_nt(h16_ref[:, l0:l0 + HALF_LANES], s5mat_ref[S5_C_RE, hf])
            + _dot_nt(h16_ref[:, PART_LANES + l0:PART_LANES + l0 + HALF_LANES],
                      s5mat_ref[S5_C_IM_NEG, hf]))
    s = jnp.concatenate(y_halves, axis=-1) + d_skip * ub_ref[...]

    gl = jax.nn.gelu(s).astype(BF16)
    branch_b = _dot(gl, w_glu_v_ref[...]) * _sigmoid(_dot(gl, w_glu_g_ref[...]))
    merged = (_sigmoid(gates_ref[:, 0:D_MODEL]) * ya_ref[...]
              + _sigmoid(gates_ref[:, D_MODEL:2 * D_MODEL]) * branch_b).astype(BF16)
    x1 = x + _dot(merged, w_out_ref[...])
    if stream_io:
        x1t_ref[slot] = x1.reshape(t_chunk, n_seq, D_MODEL)
        for c in store_x1(step, slot):
            c.start()

        @pl.when(step == last_step)
        def _drain_out():
            if t_chunk * 2 <= x_ref.shape[1]:
                for c in store_x1(step - 1, 1 - slot):
                    c.wait()
            for c in store_x1(step, slot):
                c.wait()
    else:
        x1_ref[...] = x1.reshape(x1_ref.shape)

    if cast_weights:
        @pl.when(step == pl.num_programs(0) - 1)
        def _finish_exports():
            for i in range(len(big_vmem)):
                export_copy(i).wait()

    for k, (_, _, _, n, _) in enumerate(later_jobs):
        @pl.when(step == k + 1)
        def _convert_chunk(k=k, n=n):
            if k >= 2:
                send(k - 2).wait()
            fetch(k).wait()
            stage16_ref[k % 2, 0:n] = stage_ref[k % 2, 0:n].astype(BF16)
            send(k).start()
            if k + 1 < len(later_jobs):
                fetch(k + 1).start()
    if later_jobs:
        @pl.when(step == len(later_jobs) + 1)
        def _finish_sends():
            for k in range(max(0, len(later_jobs) - 2), len(later_jobs)):
                send(k).wait()


def _mixer_call(x, pool_in, h0, small_w, big_w, later_w=(), *, n_seq, t_chunk, pos_start):
    rows = n_seq * t_chunk
    stream_io = t_chunk > 1
    has_state = pool_in is not None
    cast_weights = big_w[0].dtype == F32
    if stream_io:
        n_steps = x.shape[1] // t_chunk
        x_spec = _IN_HBM
    else:
        n_steps = 1
        x_spec = pl.BlockSpec((1, n_seq, D_MODEL), lambda i: (0, 0, 0))
    keep_rows = POOL_BUF * n_seq

    in_arrays = [x]
    in_specs = [x_spec]
    if has_state:
        in_arrays += [pool_in, h0]
        in_specs += [_resident(pool_in.shape), _resident(h0.shape)]
    assert cast_weights or not later_w
    n_chunks = sum((w.shape[0] // min(CAST_ROWS, w.shape[0])) * (w.shape[1] // CAST_COLS)
                   for w in later_w)
    assert n_chunks + 2 <= n_steps or not later_w, "one chunk of later_w per grid step"
    in_arrays += list(small_w) + list(big_w) + list(later_w)
    in_specs += [_resident(w.shape) for w in small_w]
    in_specs += [_IN_HBM if cast_weights else _resident(w.shape) for w in big_w]
    in_specs += [_IN_HBM for _ in later_w]

    out_shape = [
        jax.ShapeDtypeStruct(x.shape, F32),
        jax.ShapeDtypeStruct((keep_rows, POOL_WIDTH), F32),
        jax.ShapeDtypeStruct((n_seq, PART_LANES), F32),
        jax.ShapeDtypeStruct((n_seq, PART_LANES), F32),
    ]
    out_specs = [
        x_spec,
        pl.BlockSpec((keep_rows, POOL_WIDTH), lambda i: (0, 0)),
        pl.BlockSpec((n_seq, PART_LANES), lambda i: (0, 0)),
        pl.BlockSpec((n_seq, PART_LANES), lambda i: (0, 0)),
    ]
    scratch = [
        pltpu.VMEM((rows, 2 * D_MODEL), F32),
        pltpu.VMEM((rows, SSM_WIDTH), F32),
        pltpu.VMEM(((POOL_SLOTS + t_chunk) * n_seq, POOL_WIDTH), F32),
        pltpu.VMEM((rows, STATE_LANES), F32),
        pltpu.VMEM((rows, STATE_LANES), BF16),
        pltpu.VMEM((n_seq, STATE_LANES), F32),
        pltpu.VMEM((rows, D_MODEL), F32),
    ]
    if cast_weights:
        out_shape += [jax.ShapeDtypeStruct(w.shape, BF16) for w in list(big_w) + list(later_w)]
        out_specs += [_IN_HBM for _ in list(big_w) + list(later_w)]
        scratch += [pltpu.VMEM(w.shape, BF16) for w in big_w]
        scratch += [pltpu.VMEM((2, CAST_ROWS, CAST_COLS), F32),
                    pltpu.VMEM((2, CAST_ROWS, CAST_COLS), BF16),
                    pltpu.SemaphoreType.DMA((2,)),
                    pltpu.SemaphoreType.DMA((len(big_w),)),
                    pltpu.SemaphoreType.DMA((2,))]
    if stream_io:
        scratch += [pltpu.VMEM((2, t_chunk, n_seq, D_MODEL), F32),
                    pltpu.VMEM((2, t_chunk, n_seq, D_MODEL), F32),
                    pltpu.SemaphoreType.DMA((2, n_seq)),
                    pltpu.SemaphoreType.DMA((2, n_seq))]
    kern = functools.partial(_mixer_kernel, n_seq=n_seq, t_chunk=t_chunk, stream_io=stream_io,
                             pos_start=pos_start, has_state=has_state, cast_weights=cast_weights,
                             n_later=len(later_w))
    return pl.pallas_call(
        kern,
        grid=(n_steps,),
        in_specs=in_specs,
        out_specs=out_specs,
        out_shape=out_shape,
        scratch_shapes=scratch,
        compiler_params=pltpu.CompilerParams(
            dimension_semantics=("arbitrary",), vmem_limit_bytes=VMEM_LIMIT_BYTES),
        name=f"mixer_n{n_seq}_t{t_chunk}",
    )(*in_arrays)


def _ffn_rows(x1, p, g_ff_ref, g_ple_ref, g_final_ref, w_ff1_ref, w_ff2_ref, w_ple_ref,
              w_ple_gate_ref):
    h2 = _rmsnorm(x1, g_ff_ref[...]).astype(BF16)
    x2 = x1
    for c in range(D_FF // FF_CHUNK):
        hid = _dot(h2, w_ff1_ref[:, c * FF_CHUNK:(c + 1) * FF_CHUNK])
        hid = jnp.square(jnp.maximum(hid, 0.0)).astype(BF16)
        x2 = x2 + _dot(hid, w_ff2_ref[c * FF_CHUNK:(c + 1) * FF_CHUNK, :])
    h3 = _rmsnorm(x2, g_ple_ref[...]).astype(BF16)
    gate = _sigmoid(_dot(h3, w_ple_gate_ref[...]))
    x3 = x2 + _dot(p.astype(BF16), w_ple_ref[...]) * gate
    return _rmsnorm(x3, g_final_ref[...])


def _ffn_kernel(x1_ref, p_ref, x1s_ref, ps_ref, *refs):
    params, (out_ref, outs_ref) = refs[:-2], refs[-2:]
    out_ref[...] = _ffn_rows(x1_ref[...], p_ref[...], *params)

    @pl.when(pl.program_id(0) == pl.num_programs(0) - 1)
    def _sample_rows():
        outs_ref[...] = _ffn_rows(x1s_ref[...], ps_ref[...], *params)


def _ffn_call(x1, p, x1s, ps, weights, *, block_rows):
    n_rows = x1.shape[0]
    in_specs = [pl.BlockSpec((block_rows, D_MODEL), lambda i: (i, 0)),
                pl.BlockSpec((block_rows, PLE_DIM), lambda i: (i, 0)),
                _resident(x1s.shape), _resident(ps.shape)]
    in_specs += [_resident(w.shape) for w in weights]
    return pl.pallas_call(
        _ffn_kernel,
        grid=(n_rows // block_rows,),
        in_specs=in_specs,
        out_specs=(pl.BlockSpec((block_rows, D_MODEL), lambda i: (i, 0)),
                   pl.BlockSpec(x1s.shape, lambda i: (0, 0))),
        out_shape=(jax.ShapeDtypeStruct((n_rows, D_MODEL), F32),
                   jax.ShapeDtypeStruct(x1s.shape, F32)),
        compiler_params=pltpu.CompilerParams(
            dimension_semantics=("arbitrary",), vmem_limit_bytes=VMEM_LIMIT_BYTES),
        name="ffn",
    )(x1, p, x1s, ps, *weights)


def _s5_params(lam_re, lam_im, log_dt, b_re, b_im, c_re, c_im):
    dt = jnp.exp(log_dt)[:, None]
    mag = jnp.exp(lam_re * dt)
    ang = lam_im * dt
    abar_re = mag * jnp.cos(ang)
    abar_im = mag * jnp.sin(ang)
    den = lam_re * lam_re + lam_im * lam_im
    nr = abar_re - 1.0
    ni = abar_im
    k_re = ((nr * lam_re + ni * lam_im) / den)[:, None, :]
    k_im = ((ni * lam_re - nr * lam_im) / den)[:, None, :]
    b_re_t = jnp.swapaxes(b_re, 1, 2)
    b_im_t = jnp.swapaxes(b_im, 1, 2)
    bbar_re = k_re * b_re_t - k_im * b_im_t
    bbar_im = k_re * b_im_t + k_im * b_re_t
    abar = jnp.concatenate([abar_re.reshape(1, PART_LANES), abar_im.reshape(1, PART_LANES)],
                           axis=1)

    compact = jnp.stack([bbar_re, bbar_im, c_re, c_im]).reshape(4 * SSM_WIDTH, SSM_STATE)
    src = lax.broadcasted_iota(jnp.int32, (SSM_STATE, HALF_LANES), 0)
    dst = lax.broadcasted_iota(jnp.int32, (SSM_STATE, HALF_LANES), 1)
    tiled = jnp.dot(compact, (src == dst % SSM_STATE).astype(F32))
    row = lax.broadcasted_iota(jnp.int32, tiled.shape, 0)
    col_group = lax.broadcasted_iota(jnp.int32, tiled.shape, 1) // SSM_STATE
    keep = (row // SSM_GROUP_DIM) % HALF_GROUPS == col_group
    signed = jnp.where(row >= S5_C_IM_NEG * SSM_WIDTH, -tiled, tiled)
    s5mat = jnp.where(keep, signed, 0.0).astype(BF16).reshape(4, SSM_HALVES, HALF_IN, HALF_LANES)
    return abar, s5mat


def kernel(x_prompt, x_sample, p_prompt, p_sample, state_pool, state_ssm_re, state_ssm_im, g_mix, w_in, w_pool, pool_scale, lam_re, lam_im, log_dt, b_re, b_im, c_re, c_im, d_skip, w_glu_v, w_glu_g, w_out, g_ff, w_ff1, w_ff2, g_ple, w_ple, w_ple_gate, g_final):
    assert w_in.shape[0] == 1, "the final norm is fused into the (single) layer's ffn call"
    batch, seq_len, _ = x_prompt.shape
    dec_batch = x_sample.shape[0]

    abar, s5mat = _s5_params(lam_re[0], lam_im[0], log_dt[0], b_re[0], b_im[0], c_re[0], c_im[0])
    small_w = (g_mix, pool_scale, d_skip, w_pool[0], abar, s5mat)

    x1p, pool_p, re_p, im_p, *w16 = _mixer_call(
        x_prompt, None, None, small_w, (w_in[0], w_glu_v[0], w_glu_g[0], w_out[0]),
        (w_ff1[0], w_ff2[0], w_ple[0], w_ple_gate[0]),
        n_seq=batch, t_chunk=PROMPT_T_CHUNK, pos_start=0)
    mixer_w16, ffn_w16 = w16[:4], w16[4:]
    pool_tm = jnp.swapaxes(state_pool[0], 0, 1).reshape(POOL_BUF * dec_batch, POOL_WIDTH)
    h0 = jnp.concatenate([state_ssm_re.reshape(dec_batch, PART_LANES),
                          state_ssm_im.reshape(dec_batch, PART_LANES)], axis=1)
    x1s, pool_s, re_s, im_s = _mixer_call(
        x_sample.reshape(1, dec_batch, D_MODEL), pool_tm, h0, small_w, mixer_w16,
        n_seq=dec_batch, t_chunk=1, pos_start=PAST_LEN)

    y_prompt, y_sample = _ffn_call(
        x1p.reshape(batch * seq_len, D_MODEL), p_prompt.reshape(batch * seq_len, PLE_DIM),
        x1s.reshape(dec_batch, D_MODEL), p_sample.reshape(dec_batch, PLE_DIM),
        [g_ff, g_ple, g_final.reshape(1, D_MODEL)] + ffn_w16, block_rows=FFN_BLOCK_ROWS)

    def to_buf(tm, n):
        return jnp.swapaxes(tm.reshape(POOL_BUF, n, POOL_WIDTH), 0, 1)[None]

    state_shape = lambda n: (1, n, SSM_GROUPS, SSM_STATE)
    return (y_prompt.reshape(batch, seq_len, D_MODEL), y_sample.reshape(dec_batch, 1, D_MODEL),
            to_buf(pool_p, batch), to_buf(pool_s, dec_batch),
            re_p.reshape(state_shape(batch)), im_p.reshape(state_shape(batch)),
            re_s.reshape(state_shape(dec_batch)), im_s.reshape(state_shape(dec_batch)))
```

```python
import functools

import jax
import jax.numpy as jnp
from jax import lax
from jax.experimental import pallas as pl
from jax.experimental.pallas import tpu as pltpu

D_MODEL = 1024
POOL_WINDOWS = (2, 4, 8, 16)
POOL_WIDTH = D_MODEL // 2
POOL_GROUP_DIM = POOL_WIDTH // len(POOL_WINDOWS)
POOL_OUT_GROUP_DIM = D_MODEL // len(POOL_WINDOWS)
POOL_BUF = max(POOL_WINDOWS) - 1
POOL_SLOTS = POOL_BUF + 1
SSM_WIDTH = D_MODEL // 2
SSM_GROUP_DIM = 16
SSM_GROUPS = SSM_WIDTH // SSM_GROUP_DIM
SSM_STATE = 64
PART_LANES = SSM_GROUPS * SSM_STATE
STATE_LANES = 2 * PART_LANES
SSM_HALVES = 2
HALF_IN = SSM_WIDTH // SSM_HALVES
HALF_GROUPS = SSM_GROUPS // SSM_HALVES
HALF_LANES = PART_LANES // SSM_HALVES
D_FF = 4 * D_MODEL
FF_CHUNK = 1024
PLE_DIM = 256
EPS = 1e-6
IN_WIDTH = POOL_WIDTH + SSM_WIDTH + 2 * D_MODEL
PAST_LEN = 16384
PROMPT_T_CHUNK = 64
FFN_BLOCK_ROWS = 1024
SCAN_LANES = 512
SCAN_UNROLL = 32
BF16_TILE_ROWS = 16
CAST_ROWS, CAST_COLS = 512, 1024
VMEM_LIMIT_BYTES = 58 * 1024 * 1024

S5_B_RE, S5_B_IM, S5_C_RE, S5_C_IM_NEG = range(4)

BF16 = jnp.bfloat16
F32 = jnp.float32


def _rmsnorm(x, g):
    return x * lax.rsqrt(jnp.mean(x * x, axis=-1, keepdims=True) + EPS) * g


def _sigmoid(x):
    return 0.5 * jnp.tanh(0.5 * x) + 0.5


def _dot(a, b):
    return jnp.dot(a, b, preferred_element_type=F32)


def _dot_nt(a, b):
    return lax.dot_general(a, b, (((1,), (1,)), ((), ())), preferred_element_type=F32)


def _resident(shape):
    nd = len(shape)
    return pl.BlockSpec(shape, lambda i, _nd=nd: (0,) * _nd, pipeline_mode=pl.Buffered(1))


_IN_HBM = pl.BlockSpec(memory_space=pl.ANY)


def _cast_jobs(pairs):
    jobs = []
    for src, dst in pairs:
        n_rows, n_cols = src.shape
        rows = min(CAST_ROWS, n_rows)
        jobs += [(src, dst, r0, rows, c0)
                 for r0 in range(0, n_rows, rows) for c0 in range(0, n_cols, CAST_COLS)]
    return jobs


def _stream_cast(pairs, stage_ref, sem):
    jobs = _cast_jobs(pairs)

    def copy(k):
        src, _, r0, rows, c0 = jobs[k]
        return pltpu.make_async_copy(src.at[r0:r0 + rows, c0:c0 + CAST_COLS],
                                     stage_ref.at[k % 2, 0:rows], sem.at[k % 2])

    copy(0).start()
    for k, (_, dst, r0, rows, c0) in enumerate(jobs):
        if k + 1 < len(jobs):
            copy(k + 1).start()
        copy(k).wait()
        dst[r0:r0 + rows, c0:c0 + CAST_COLS] = stage_ref[k % 2, 0:rows].astype(BF16)


def _mixer_kernel(*refs, n_seq, t_chunk, stream_io, pos_start, has_state, cast_weights, n_later):
    it = iter(refs)
    x_ref = next(it)
    pool_in_ref = next(it) if has_state else None
    h0_ref = next(it) if has_state else None
    g_mix_ref, pool_scale_ref, d_skip_ref, w_pool_ref, abar_ref, s5mat_ref = (
        next(it) for _ in range(6))
    big_in = [next(it) for _ in range(4)]
    later_in = [next(it) for _ in range(n_later)]
    x1_ref, ua_out_ref, h_re_out_ref, h_im_out_ref = (next(it) for _ in range(4))
    big_out = [next(it) for _ in range(4)] if cast_weights else None
    later_out = [next(it) for _ in range(n_later)]
    gates_ref, ub_ref, ext_ref, hs_ref, h16_ref, hc_ref, ya_ref = (next(it) for _ in range(7))
    if cast_weights:
        big_vmem = [next(it) for _ in range(4)]
        stage_ref, stage16_ref, cast_sem, export_sem, send_sem = (next(it) for _ in range(5))
    else:
        big_vmem = big_in
    if stream_io:
        xt_ref, x1t_ref, in_sem, out_sem = (next(it) for _ in range(4))
    else:
        xs_ref, xs_sem = (next(it) for _ in range(2))
    w_in_ref, w_glu_v_ref, w_glu_g_ref, w_out_ref = big_vmem

    rows = n_seq * t_chunk
    step = pl.program_id(0)
    last_step = pl.num_programs(0) - 1

    def load_x(i, slot):
        return [pltpu.make_async_copy(x_ref.at[b, pl.ds(i * t_chunk, t_chunk), :],
                                      xt_ref.at[slot, :, b, :], in_sem.at[slot, b])
                for b in range(n_seq)]

    def store_x1(i, slot):
        return [pltpu.make_async_copy(x1t_ref.at[slot, :, b, :],
                                      x1_ref.at[b, pl.ds(i * t_chunk, t_chunk), :],
                                      out_sem.at[slot, b])
                for b in range(n_seq)]
    cur0 = POOL_SLOTS * n_seq
    g_mix = g_mix_ref[...]
    pool_scale = pool_scale_ref[...]
    d_skip = d_skip_ref[...]

    def export_copy(i):
        return pltpu.make_async_copy(big_vmem[i], big_out[i], export_sem.at[i])

    later_jobs = _cast_jobs(list(zip(later_in, later_out)))

    def fetch(k):
        src, _, r0, n, c0 = later_jobs[k]
        return pltpu.make_async_copy(src.at[r0:r0 + n, c0:c0 + CAST_COLS],
                                     stage_ref.at[k % 2, 0:n], cast_sem.at[k % 2])

    def send(k):
        _, dst, r0, n, c0 = later_jobs[k]
        return pltpu.make_async_copy(stage16_ref.at[k % 2, 0:n],
                                     dst.at[r0:r0 + n, c0:c0 + CAST_COLS], send_sem.at[k % 2])

    @pl.when(step == 0)
    def _init():
        if stream_io:
            for c in load_x(0, 0):
                c.start()
        if has_state:
            ext_ref[n_seq:cur0, :] = pool_in_ref[...]
            hc_ref[...] = h0_ref[...]
        else:
            ext_ref[0:cur0, :] = jnp.zeros((cur0, POOL_WIDTH), F32)
            hc_ref[...] = jnp.zeros((n_seq, STATE_LANES), F32)
        if cast_weights:
            _stream_cast(list(zip(big_in, big_vmem)), stage_ref, cast_sem)
            for i in range(len(big_vmem)):
                export_copy(i).start()
            if later_jobs:
                fetch(0).start()

    if stream_io:
        slot = lax.rem(step, 2)

        @pl.when(step < last_step)
        def _prefetch():
            for c in load_x(step + 1, 1 - slot):
                c.start()

        for c in load_x(step, slot):
            c.wait()

        @pl.when(step >= 2)
        def _free_out_slot():
            for c in store_x1(step - 2, slot):
                c.wait()

        x = xt_ref[slot].reshape(rows, D_MODEL)
    else:
        x_copy = pltpu.make_async_copy(x_ref.at[:, 0, :], xs_ref, xs_sem.at[0])
        x_copy.start()
        x_copy.wait()
        x = xs_ref[...]
    h = _rmsnorm(x, g_mix).astype(BF16)
    g0 = POOL_WIDTH + SSM_WIDTH
    u = _dot(h, w_in_ref[:, 0:g0])
    ext_ref[cur0:cur0 + rows, :] = u[:, 0:POOL_WIDTH]
    ub_ref[...] = u[:, POOL_WIDTH:g0]
    u_b16 = u[:, POOL_WIDTH:g0].astype(BF16)
    for hf in range(SSM_HALVES):
        u_half = u_b16[:, hf * HALF_IN:(hf + 1) * HALF_IN]
        l0 = hf * HALF_LANES
        hs_ref[:, l0:l0 + HALF_LANES] = _dot(u_half, s5mat_ref[S5_B_RE, hf])
        hs_ref[:, PART_LANES + l0:PART_LANES + l0 + HALF_LANES] = _dot(
            u_half, s5mat_ref[S5_B_IM, hf])
    gates_ref[...] = _dot(h, w_in_ref[:, g0:IN_WIDTH])

    t_idx = lax.shift_right_logical(
        lax.broadcasted_iota(jnp.int32, (rows, POOL_GROUP_DIM), 0), n_seq.bit_length() - 1)
    pos = t_idx + (pos_start + step * t_chunk)
    for gi, w in enumerate(POOL_WINDOWS):
        c0 = gi * POOL_GROUP_DIM
        cur = ext_ref[cur0:cur0 + rows, c0:c0 + POOL_GROUP_DIM]
        acc = cur
        for k in range(1, w):
            r0 = cur0 - k * n_seq
            acc = acc + ext_ref[r0:r0 + rows, c0:c0 + POOL_GROUP_DIM]
        count = jnp.minimum(pos + 1, w).astype(F32)
        pooled = (acc / count - cur).astype(BF16)
        o0 = gi * POOL_OUT_GROUP_DIM
        ya_ref[:, o0:o0 + POOL_OUT_GROUP_DIM] = (
            _dot(pooled, w_pool_ref[gi].astype(BF16))
            * pool_scale[:, o0:o0 + POOL_OUT_GROUP_DIM])
    ua_out_ref[...] = ext_ref[cur0 + rows - POOL_BUF * n_seq:cur0 + rows, :]
    if t_chunk >= POOL_SLOTS:
        ext_ref[0:cur0, :] = ext_ref[rows:rows + cur0, :]

    steps_per_store = max(1, BF16_TILE_ROWS // n_seq)
    store_rows = steps_per_store * n_seq
    for q in range(PART_LANES // SCAN_LANES):
        re0 = q * SCAN_LANES
        im0 = PART_LANES + re0
        a_re = abar_ref[:, re0:re0 + SCAN_LANES]
        a_im = abar_ref[:, im0:im0 + SCAN_LANES]
        if t_chunk > 1:
            a_re = jnp.broadcast_to(a_re, (n_seq, SCAN_LANES))
            a_im = jnp.broadcast_to(a_im, (n_seq, SCAN_LANES))

        def scan_steps(i, carry, re0=re0, im0=im0, a_re=a_re, a_im=a_im):
            h_re, h_im = carry
            r0 = i * store_rows if isinstance(i, int) else pl.multiple_of(i * store_rows,
                                                                          store_rows)
            new_re, new_im = [], []
            for s in range(steps_per_store):
                rs = r0 + s * n_seq
                h_re, h_im = (
                    a_re * h_re - a_im * h_im + hs_ref[pl.ds(rs, n_seq), re0:re0 + SCAN_LANES],
                    a_re * h_im + a_im * h_re + hs_ref[pl.ds(rs, n_seq), im0:im0 + SCAN_LANES])
                new_re.append(h_re)
                new_im.append(h_im)
            h16_ref[pl.ds(r0, store_rows), re0:re0 + SCAN_LANES] = (
                jnp.concatenate(new_re, axis=0).astype(BF16))
            h16_ref[pl.ds(r0, store_rows), im0:im0 + SCAN_LANES] = (
                jnp.concatenate(new_im, axis=0).astype(BF16))
            return h_re, h_im

        carry = (hc_ref[:, re0:re0 + SCAN_LANES], hc_ref[:, im0:im0 + SCAN_LANES])
        if t_chunk == steps_per_store:
            carry = scan_steps(0, carry)
        else:
            carry = lax.fori_loop(0, t_chunk // steps_per_store, scan_steps, carry,
                                  unroll=SCAN_UNROLL)
        hc_ref[:, re0:re0 + SCAN_LANES] = carry[0]
        hc_ref[:, im0:im0 + SCAN_LANES] = carry[1]
    h_re_out_ref[...] = hc_ref[:, 0:PART_LANES]
    h_im_out_ref[...] = hc_ref[:, PART_LANES:STATE_LANES]

    y_halves = []
    for hf in range(SSM_HALVES):
        l0 = hf * HALF_LANES
        y_halves.append(
            _dot_nt(h16_ref[:, l0:l0 + HALF_LANES], s5mat_ref[S5_C_RE, hf])
            + _dot_nt(h16_ref[:, PART_LANES + l0:PART_LANES + l0 + HALF_LANES],
                      s5mat_ref[S5_C_IM_NEG, hf]))
    s = jnp.concatenate(y_halves, axis=-1) + d_skip * ub_ref[...]

    gl = jax.nn.gelu(s).astype(BF16)
    branch_b = _dot(gl, w_glu_v_ref[...]) * _sigmoid(_dot(gl, w_glu_g_ref[...]))
    merged = (_sigmoid(gates_ref[:, 0:D_MODEL]) * ya_ref[...]
              + _sigmoid(gates_ref[:, D_MODEL:2 * D_MODEL]) * branch_b).astype(BF16)
    x1 = x + _dot(merged, w_out_ref[...])
    if stream_io:
        x1t_ref[slot] = x1.reshape(t_chunk, n_seq, D_MODEL)
        for c in store_x1(step, slot):
            c.start()

        @pl.when(step == last_step)
        def _drain_out():
            if t_chunk * 2 <= x_ref.shape[1]:
                for c in store_x1(step - 1, 1 - slot):
                    c.wait()
            for c in store_x1(step, slot):
                c.wait()
    else:
        x1_ref[...] = x1.reshape(x1_ref.shape)

    if cast_weights:
        @pl.when(step == pl.num_programs(0) - 1)
        def _finish_exports():
            for i in range(len(big_vmem)):
                export_copy(i).wait()

    for k, (_, _, _, n, _) in enumerate(later_jobs):
        @pl.when(step == k + 1)
        def _convert_chunk(k=k, n=n):
            if k >= 2:
                send(k - 2).wait()
            fetch(k).wait()
            stage16_ref[k % 2, 0:n] = stage_ref[k % 2, 0:n].astype(BF16)
            send(k).start()
            if k + 1 < len(later_jobs):
                fetch(k + 1).start()
    if later_jobs:
        @pl.when(step == len(later_jobs) + 1)
        def _finish_sends():
            for k in range(max(0, len(later_jobs) - 2), len(later_jobs)):
                send(k).wait()


def _mixer_call(x, pool_in, h0, small_w, big_w, later_w=(), *, n_seq, t_chunk, pos_start):
    rows = n_seq * t_chunk
    stream_io = t_chunk > 1
    has_state = pool_in is not None
    cast_weights = big_w[0].dtype == F32
    if stream_io:
        n_steps = x.shape[1] // t_chunk
        x1_shape, x1_spec = x.shape, _IN_HBM
    else:
        n_steps = 1
        x1_shape, x1_spec = (n_seq, D_MODEL), pl.BlockSpec((n_seq, D_MODEL), lambda i: (0, 0))
    keep_rows = POOL_BUF * n_seq

    in_arrays = [x]
    in_specs = [_IN_HBM]
    if has_state:
        in_arrays += [pool_in, h0]
        in_specs += [_resident(pool_in.shape), _resident(h0.shape)]
    assert cast_weights or not later_w
    n_chunks = sum((w.shape[0] // min(CAST_ROWS, w.shape[0])) * (w.shape[1] // CAST_COLS)
                   for w in later_w)
    assert n_chunks + 2 <= n_steps or not later_w, "one chunk of later_w per grid step"
    in_arrays += list(small_w) + list(big_w) + list(later_w)
    in_specs += [_resident(w.shape) for w in small_w]
    in_specs += [_IN_HBM if cast_weights else _resident(w.shape) for w in big_w]
    in_specs += [_IN_HBM for _ in later_w]

    out_shape = [
        jax.ShapeDtypeStruct(x1_shape, F32),
        jax.ShapeDtypeStruct((keep_rows, POOL_WIDTH), F32),
        jax.ShapeDtypeStruct((n_seq, PART_LANES), F32),
        jax.ShapeDtypeStruct((n_seq, PART_LANES), F32),
    ]
    out_specs = [
        x1_spec,
        pl.BlockSpec((keep_rows, POOL_WIDTH), lambda i: (0, 0)),
        pl.BlockSpec((n_seq, PART_LANES), lambda i: (0, 0)),
        pl.BlockSpec((n_seq, PART_LANES), lambda i: (0, 0)),
    ]
    scratch = [
        pltpu.VMEM((rows, 2 * D_MODEL), F32),
        pltpu.VMEM((rows, SSM_WIDTH), F32),
        pltpu.VMEM(((POOL_SLOTS + t_chunk) * n_seq, POOL_WIDTH), F32),
        pltpu.VMEM((rows, STATE_LANES), F32),
        pltpu.VMEM((rows, STATE_LANES), BF16),
        pltpu.VMEM((n_seq, STATE_LANES), F32),
        pltpu.VMEM((rows, D_MODEL), F32),
    ]
    if cast_weights:
        out_shape += [jax.ShapeDtypeStruct(w.shape, BF16) for w in list(big_w) + list(later_w)]
        out_specs += [_IN_HBM for _ in list(big_w) + list(later_w)]
        scratch += [pltpu.VMEM(w.shape, BF16) for w in big_w]
        scratch += [pltpu.VMEM((2, CAST_ROWS, CAST_COLS), F32),
                    pltpu.VMEM((2, CAST_ROWS, CAST_COLS), BF16),
                    pltpu.SemaphoreType.DMA((2,)),
                    pltpu.SemaphoreType.DMA((len(big_w),)),
                    pltpu.SemaphoreType.DMA((2,))]
    if stream_io:
        scratch += [pltpu.VMEM((2, t_chunk, n_seq, D_MODEL), F32),
                    pltpu.VMEM((2, t_chunk, n_seq, D_MODEL), F32),
                    pltpu.SemaphoreType.DMA((2, n_seq)),
                    pltpu.SemaphoreType.DMA((2, n_seq))]
    else:
        scratch += [pltpu.VMEM((n_seq, D_MODEL), F32), pltpu.SemaphoreType.DMA((1,))]
    kern = functools.partial(_mixer_kernel, n_seq=n_seq, t_chunk=t_chunk, stream_io=stream_io,
                             pos_start=pos_start, has_state=has_state, cast_weights=cast_weights,
                             n_later=len(later_w))
    return pl.pallas_call(
        kern,
        grid=(n_steps,),
        in_specs=in_specs,
        out_specs=out_specs,
        out_shape=out_shape,
        scratch_shapes=scratch,
        compiler_params=pltpu.CompilerParams(
            dimension_semantics=("arbitrary",), vmem_limit_bytes=VMEM_LIMIT_BYTES),
        name=f"mixer_n{n_seq}_t{t_chunk}",
    )(*in_arrays)


def _ffn_rows(x1, p, g_ff_ref, g_ple_ref, g_final_ref, w_ff1_ref, w_ff2_ref, w_ple_ref,
              w_ple_gate_ref):
    h2 = _rmsnorm(x1, g_ff_ref[...]).astype(BF16)
    x2 = x1
    for c in range(D_FF // FF_CHUNK):
        hid = _dot(h2, w_ff1_ref[:, c * FF_CHUNK:(c + 1) * FF_CHUNK])
        hid = jnp.square(jnp.maximum(hid, 0.0)).astype(BF16)
        x2 = x2 + _dot(hid, w_ff2_ref[c * FF_CHUNK:(c + 1) * FF_CHUNK, :])
    h3 = _rmsnorm(x2, g_ple_ref[...]).astype(BF16)
    gate = _sigmoid(_dot(h3, w_ple_gate_ref[...]))
    x3 = x2 + _dot(p.astype(BF16), w_ple_ref[...]) * gate
    return _rmsnorm(x3, g_final_ref[...])


def _ffn_kernel(x1_ref, p_ref, x1s_ref, ps_hbm, *refs):
    params = refs[:7]
    out_ref, outs_hbm, ps_ref, ys_ref, sem = refs[7:]
    out_ref[...] = _ffn_rows(x1_ref[...], p_ref[...], *params)

    @pl.when(pl.program_id(0) == pl.num_programs(0) - 1)
    def _sample_rows():
        p_copy = pltpu.make_async_copy(ps_hbm.at[:, 0, :], ps_ref, sem.at[0])
        p_copy.start()
        p_copy.wait()
        ys_ref[...] = _ffn_rows(x1s_ref[...], ps_ref[...], *params)
        y_copy = pltpu.make_async_copy(ys_ref, outs_hbm.at[:, 0, :], sem.at[1])
        y_copy.start()
        y_copy.wait()


def _ffn_call(x1, p, x1s, ps, weights, *, block_rows):
    n_rows = x1.shape[0]
    n_s = x1s.shape[0]
    in_specs = [pl.BlockSpec((block_rows, D_MODEL), lambda i: (i, 0)),
                pl.BlockSpec((block_rows, PLE_DIM), lambda i: (i, 0)),
                _resident(x1s.shape), _IN_HBM]
    in_specs += [_resident(w.shape) for w in weights]
    return pl.pallas_call(
        _ffn_kernel,
        grid=(n_rows // block_rows,),
        in_specs=in_specs,
        out_specs=(pl.BlockSpec((block_rows, D_MODEL), lambda i: (i, 0)), _IN_HBM),
        out_shape=(jax.ShapeDtypeStruct((n_rows, D_MODEL), F32),
                   jax.ShapeDtypeStruct((n_s, 1, D_MODEL), F32)),
        scratch_shapes=[pltpu.VMEM((n_s, PLE_DIM), F32), pltpu.VMEM((n_s, D_MODEL), F32),
                        pltpu.SemaphoreType.DMA((2,))],
        compiler_params=pltpu.CompilerParams(
            dimension_semantics=("arbitrary",), vmem_limit_bytes=VMEM_LIMIT_BYTES),
        name="ffn",
    )(x1, p, x1s, ps, *weights)


def _s5_params(lam_re, lam_im, log_dt, b_re, b_im, c_re, c_im):
    dt = jnp.exp(log_dt)[:, None]
    mag = jnp.exp(lam_re * dt)
    ang = lam_im * dt
    abar_re = mag * jnp.cos(ang)
    abar_im = mag * jnp.sin(ang)
    den = lam_re * lam_re + lam_im * lam_im
    nr = abar_re - 1.0
    ni = abar_im
    k_re = ((nr * lam_re + ni * lam_im) / den)[:, None, :]
    k_im = ((ni * lam_re - nr * lam_im) / den)[:, None, :]
    b_re_t = jnp.swapaxes(b_re, 1, 2)
    b_im_t = jnp.swapaxes(b_im, 1, 2)
    bbar_re = k_re * b_re_t - k_im * b_im_t
    bbar_im = k_re * b_im_t + k_im * b_re_t
    abar = jnp.concatenate([abar_re.reshape(1, PART_LANES), abar_im.reshape(1, PART_LANES)],
                           axis=1)

    compact = jnp.stack([bbar_re, bbar_im, c_re, c_im]).reshape(4 * SSM_WIDTH, SSM_STATE)
    src = lax.broadcasted_iota(jnp.int32, (SSM_STATE, HALF_LANES), 0)
    dst = lax.broadcasted_iota(jnp.int32, (SSM_STATE, HALF_LANES), 1)
    tiled = jnp.dot(compact, (src == dst % SSM_STATE).astype(F32))
    row = lax.broadcasted_iota(jnp.int32, tiled.shape, 0)
    col_group = lax.broadcasted_iota(jnp.int32, tiled.shape, 1) // SSM_STATE
    keep = (row // SSM_GROUP_DIM) % HALF_GROUPS == col_group
    signed = jnp.where(row >= S5_C_IM_NEG * SSM_WIDTH, -tiled, tiled)
    s5mat = jnp.where(keep, signed, 0.0).astype(BF16).reshape(4, SSM_HALVES, HALF_IN, HALF_LANES)
    return abar, s5mat


def kernel(x_prompt, x_sample, p_prompt, p_sample, state_pool, state_ssm_re, state_ssm_im, g_mix, w_in, w_pool, pool_scale, lam_re, lam_im, log_dt, b_re, b_im, c_re, c_im, d_skip, w_glu_v, w_glu_g, w_out, g_ff, w_ff1, w_ff2, g_ple, w_ple, w_ple_gate, g_final):
    assert w_in.shape[0] == 1, "the final norm is fused into the (single) layer's ffn call"
    batch, seq_len, _ = x_prompt.shape
    dec_batch = x_sample.shape[0]

    abar, s5mat = _s5_params(lam_re[0], lam_im[0], log_dt[0], b_re[0], b_im[0], c_re[0], c_im[0])
    small_w = (g_mix, pool_scale, d_skip, w_pool[0], abar, s5mat)

    x1p, pool_p, re_p, im_p, *w16 = _mixer_call(
        x_prompt, None, None, small_w, (w_in[0], w_glu_v[0], w_glu_g[0], w_out[0]),
        (w_ff1[0], w_ff2[0], w_ple[0], w_ple_gate[0]),
        n_seq=batch, t_chunk=PROMPT_T_CHUNK, pos_start=0)
    mixer_w16, ffn_w16 = w16[:4], w16[4:]
    pool_tm = jnp.swapaxes(state_pool[0], 0, 1).reshape(POOL_BUF * dec_batch, POOL_WIDTH)
    h0 = jnp.concatenate([state_ssm_re.reshape(dec_batch, PART_LANES),
                          state_ssm_im.reshape(dec_batch, PART_LANES)], axis=1)
    x1s, pool_s, re_s, im_s = _mixer_call(
        x_sample, pool_tm, h0, small_w, mixer_w16, n_seq=dec_batch, t_chunk=1, pos_start=PAST_LEN)

    y_prompt, y_sample = _ffn_call(
        x1p.reshape(batch * seq_len, D_MODEL), p_prompt.reshape(batch * seq_len, PLE_DIM),
        x1s, p_sample[0], [g_ff, g_ple, g_final.reshape(1, D_MODEL)] + ffn_w16,
        block_rows=FFN_BLOCK_ROWS)

    def to_buf(tm, n):
        return jnp.swapaxes(tm.reshape(POOL_BUF, n, POOL_WIDTH), 0, 1)[None]

    state_shape = lambda n: (1, n, SSM_GROUPS, SSM_STATE)
    return (y_prompt.reshape(batch, seq_len, D_MODEL), y_sample,
            to_buf(pool_p, batch), to_buf(pool_s, dec_batch),
            re_p.reshape(state_shape(batch)), im_p.reshape(state_shape(batch)),
            re_s.reshape(state_shape(dec_batch)), im_s.reshape(state_shape(dec_batch)))
```

```python
import functools

import jax
import jax.numpy as jnp
from jax import lax
from jax.experimental import pallas as pl
from jax.experimental.pallas import tpu as pltpu

D_MODEL = 1024
POOL_WINDOWS = (2, 4, 8, 16)
POOL_WIDTH = D_MODEL // 2
POOL_GROUP_DIM = POOL_WIDTH // len(POOL_WINDOWS)
POOL_OUT_GROUP_DIM = D_MODEL // len(POOL_WINDOWS)
POOL_BUF = max(POOL_WINDOWS) - 1
POOL_SLOTS = POOL_BUF + 1
SSM_WIDTH = D_MODEL // 2
SSM_GROUP_DIM = 16
SSM_GROUPS = SSM_WIDTH // SSM_GROUP_DIM
SSM_STATE = 64
PART_LANES = SSM_GROUPS * SSM_STATE
STATE_LANES = 2 * PART_LANES
SSM_HALVES = 2
HALF_IN = SSM_WIDTH // SSM_HALVES
HALF_GROUPS = SSM_GROUPS // SSM_HALVES
HALF_LANES = PART_LANES // SSM_HALVES
D_FF = 4 * D_MODEL
FF_CHUNK = 1024
PLE_DIM = 256
EPS = 1e-6
IN_WIDTH = POOL_WIDTH + SSM_WIDTH + 2 * D_MODEL
PAST_LEN = 16384
PROMPT_T_CHUNK = 64
FFN_BLOCK_ROWS = 1024
SCAN_LANES = 512
SCAN_UNROLL = 32
BF16_TILE_ROWS = 16
CAST_ROWS, CAST_COLS = 512, 1024
VMEM_LIMIT_BYTES = 58 * 1024 * 1024

S5_B_RE, S5_B_IM, S5_C_RE, S5_C_IM_NEG = range(4)

BF16 = jnp.bfloat16
F32 = jnp.float32


def _rmsnorm(x, g):
    return x * lax.rsqrt(jnp.mean(x * x, axis=-1, keepdims=True) + EPS) * g


def _sigmoid(x):
    return 0.5 * jnp.tanh(0.5 * x) + 0.5


def _dot(a, b):
    return jnp.dot(a, b, preferred_element_type=F32)


def _dot_nt(a, b):
    return lax.dot_general(a, b, (((1,), (1,)), ((), ())), preferred_element_type=F32)


def _resident(shape):
    nd = len(shape)
    return pl.BlockSpec(shape, lambda i, _nd=nd: (0,) * _nd, pipeline_mode=pl.Buffered(1))


_IN_HBM = pl.BlockSpec(memory_space=pl.ANY)


def _cast_jobs(pairs):
    jobs = []
    for src, dst in pairs:
        n_rows, n_cols = src.shape
        rows = min(CAST_ROWS, n_rows)
        jobs += [(src, dst, r0, rows, c0)
                 for r0 in range(0, n_rows, rows) for c0 in range(0, n_cols, CAST_COLS)]
    return jobs


def _stream_cast(pairs, stage_ref, sem, meanwhile=None):
    jobs = _cast_jobs(pairs)

    def copy(k):
        src, _, r0, rows, c0 = jobs[k]
        return pltpu.make_async_copy(src.at[r0:r0 + rows, c0:c0 + CAST_COLS],
                                     stage_ref.at[k % 2, 0:rows], sem.at[k % 2])

    for k in range(min(2, len(jobs))):
        copy(k).start()
    if meanwhile is not None:
        meanwhile()
    for k, (_, dst, r0, rows, c0) in enumerate(jobs):
        copy(k).wait()
        dst[r0:r0 + rows, c0:c0 + CAST_COLS] = stage_ref[k % 2, 0:rows].astype(BF16)
        if k + 2 < len(jobs):
            copy(k + 2).start()


def _mixer_kernel(*refs, n_seq, t_chunk, stream_io, pos_start, has_state, cast_weights, n_later):
    it = iter(refs)
    x_ref = next(it)
    pool_in_ref = next(it) if has_state else None
    h0_re_ref, h0_im_ref = (next(it), next(it)) if has_state else (None, None)
    g_mix_ref, pool_scale_ref, d_skip_ref, w_pool_ref, abar_ref, s5c_ref = (
        next(it) for _ in range(6))
    big_in = [next(it) for _ in range(4)]
    later_in = [next(it) for _ in range(n_later)]
    x1_ref, ua_out_ref, h_re_out_ref, h_im_out_ref = (next(it) for _ in range(4))
    big_out = [next(it) for _ in range(4)] if cast_weights else None
    later_out = [next(it) for _ in range(n_later)]
    gates_ref, ub_ref, ext_ref, hs_ref, h16_ref, hc_ref, ya_ref, s5mat_ref = (
        next(it) for _ in range(8))
    if cast_weights:
        big_vmem = [next(it) for _ in range(4)]
        stage_ref, stage16_ref, cast_sem, export_sem, send_sem = (next(it) for _ in range(5))
    else:
        big_vmem = big_in
    if stream_io:
        xt_ref, x1t_ref, in_sem, out_sem = (next(it) for _ in range(4))
    else:
        xs_ref, xs_sem = (next(it) for _ in range(2))
    w_in_ref, w_glu_v_ref, w_glu_g_ref, w_out_ref = big_vmem

    rows = n_seq * t_chunk
    step = pl.program_id(0)
    last_step = pl.num_programs(0) - 1

    def load_x(i, slot):
        return [pltpu.make_async_copy(x_ref.at[b, pl.ds(i * t_chunk, t_chunk), :],
                                      xt_ref.at[slot, :, b, :], in_sem.at[slot, b])
                for b in range(n_seq)]

    def store_x1(i, slot):
        return [pltpu.make_async_copy(x1t_ref.at[slot, :, b, :],
                                      x1_ref.at[b, pl.ds(i * t_chunk, t_chunk), :],
                                      out_sem.at[slot, b])
                for b in range(n_seq)]
    cur0 = POOL_SLOTS * n_seq
    g_mix = g_mix_ref[...]
    pool_scale = pool_scale_ref[...]
    d_skip = d_skip_ref[...]

    def export_copy(i):
        return pltpu.make_async_copy(big_vmem[i], big_out[i], export_sem.at[i])

    later_jobs = _cast_jobs(list(zip(later_in, later_out)))

    def fetch(k):
        src, _, r0, n, c0 = later_jobs[k]
        return pltpu.make_async_copy(src.at[r0:r0 + n, c0:c0 + CAST_COLS],
                                     stage_ref.at[k % 2, 0:n], cast_sem.at[k % 2])

    def send(k):
        _, dst, r0, n, c0 = later_jobs[k]
        return pltpu.make_async_copy(stage16_ref.at[k % 2, 0:n],
                                     dst.at[r0:r0 + n, c0:c0 + CAST_COLS], send_sem.at[k % 2])

    def build_s5mat():
        lane = lax.broadcasted_iota(jnp.int32, (SSM_STATE, HALF_LANES), 1)
        tile16 = (lax.broadcasted_iota(jnp.int32, (SSM_STATE, HALF_LANES), 0)
                  == jnp.bitwise_and(lane, SSM_STATE - 1)).astype(BF16)
        row_g = lax.shift_right_logical(
            lax.broadcasted_iota(jnp.int32, (HALF_IN, HALF_LANES), 0), SSM_GROUP_DIM.bit_length() - 1)
        col_g = lax.shift_right_logical(
            lax.broadcasted_iota(jnp.int32, (HALF_IN, HALF_LANES), 1), SSM_STATE.bit_length() - 1)
        for which in range(4):
            for hf in range(SSM_HALVES):
                r0 = (which * SSM_HALVES + hf) * HALF_IN
                blk = _dot(s5c_ref[r0:r0 + HALF_IN, :].astype(BF16), tile16)
                blk = jnp.where(row_g == col_g, blk, 0.0)
                s5mat_ref[which, hf] = (-blk if which == S5_C_IM_NEG else blk).astype(BF16)

    @pl.when(step == 0)
    def _init():
        if stream_io:
            for c in load_x(0, 0):
                c.start()
        if has_state:
            ext_ref[n_seq:cur0, :] = pool_in_ref[...]
            hc_ref[:, 0:PART_LANES] = h0_re_ref[...]
            hc_ref[:, PART_LANES:STATE_LANES] = h0_im_ref[...]
        else:
            ext_ref[0:cur0, :] = jnp.zeros((cur0, POOL_WIDTH), F32)
            hc_ref[...] = jnp.zeros((n_seq, STATE_LANES), F32)
        if cast_weights:
            _stream_cast(list(zip(big_in, big_vmem)), stage_ref, cast_sem, meanwhile=build_s5mat)
            for i in range(len(big_vmem)):
                export_copy(i).start()
            if later_jobs:
                fetch(0).start()
        else:
            build_s5mat()

    if stream_io:
        slot = lax.rem(step, 2)

        @pl.when(step < last_step)
        def _prefetch():
            for c in load_x(step + 1, 1 - slot):
                c.start()

        for c in load_x(step, slot):
            c.wait()

        @pl.when(step >= 2)
        def _free_out_slot():
            for c in store_x1(step - 2, slot):
                c.wait()

        x = xt_ref[slot].reshape(rows, D_MODEL)
    else:
        x_copy = pltpu.make_async_copy(x_ref.at[:, 0, :], xs_ref, xs_sem.at[0])
        x_copy.start()
        x_copy.wait()
        x = xs_ref[...]
    h = _rmsnorm(x, g_mix).astype(BF16)
    g0 = POOL_WIDTH + SSM_WIDTH
    u = _dot(h, w_in_ref[:, 0:g0])
    ext_ref[cur0:cur0 + rows, :] = u[:, 0:POOL_WIDTH]
    ub_ref[...] = u[:, POOL_WIDTH:g0]
    u_b16 = u[:, POOL_WIDTH:g0].astype(BF16)
    for hf in range(SSM_HALVES):
        u_half = u_b16[:, hf * HALF_IN:(hf + 1) * HALF_IN]
        l0 = hf * HALF_LANES
        hs_ref[:, l0:l0 + HALF_LANES] = _dot(u_half, s5mat_ref[S5_B_RE, hf])
        hs_ref[:, PART_LANES + l0:PART_LANES + l0 + HALF_LANES] = _dot(
            u_half, s5mat_ref[S5_B_IM, hf])
    gates_ref[...] = _dot(h, w_in_ref[:, g0:IN_WIDTH])

    t_idx = lax.shift_right_logical(
        lax.broadcasted_iota(jnp.int32, (rows, POOL_GROUP_DIM), 0), n_seq.bit_length() - 1)
    pos = t_idx + (pos_start + step * t_chunk)
    for gi, w in enumerate(POOL_WINDOWS):
        c0 = gi * POOL_GROUP_DIM
        cur = ext_ref[cur0:cur0 + rows, c0:c0 + POOL_GROUP_DIM]
        acc = cur
        for k in range(1, w):
            r0 = cur0 - k * n_seq
            acc = acc + ext_ref[r0:r0 + rows, c0:c0 + POOL_GROUP_DIM]
        count = jnp.minimum(pos + 1, w).astype(F32)
        pooled = (acc / count - cur).astype(BF16)
        o0 = gi * POOL_OUT_GROUP_DIM
        ya_ref[:, o0:o0 + POOL_OUT_GROUP_DIM] = (
            _dot(pooled, w_pool_ref[gi].astype(BF16))
            * pool_scale[:, o0:o0 + POOL_OUT_GROUP_DIM])
    ua_out_ref[...] = ext_ref[cur0 + rows - POOL_BUF * n_seq:cur0 + rows, :]
    if t_chunk >= POOL_SLOTS:
        ext_ref[0:cur0, :] = ext_ref[rows:rows + cur0, :]

    steps_per_store = max(1, BF16_TILE_ROWS // n_seq)
    store_rows = steps_per_store * n_seq
    for q in range(PART_LANES // SCAN_LANES):
        re0 = q * SCAN_LANES
        im0 = PART_LANES + re0
        a_re = abar_ref[:, re0:re0 + SCAN_LANES]
        a_im = abar_ref[:, im0:im0 + SCAN_LANES]
        if t_chunk > 1:
            a_re = jnp.broadcast_to(a_re, (n_seq, SCAN_LANES))
            a_im = jnp.broadcast_to(a_im, (n_seq, SCAN_LANES))

        def scan_steps(i, carry, re0=re0, im0=im0, a_re=a_re, a_im=a_im):
            h_re, h_im = carry
            r0 = i * store_rows if isinstance(i, int) else pl.multiple_of(i * store_rows,
                                                                          store_rows)
            new_re, new_im = [], []
            for s in range(steps_per_store):
                rs = r0 + s * n_seq
                h_re, h_im = (
                    a_re * h_re - a_im * h_im + hs_ref[pl.ds(rs, n_seq), re0:re0 + SCAN_LANES],
                    a_re * h_im + a_im * h_re + hs_ref[pl.ds(rs, n_seq), im0:im0 + SCAN_LANES])
                new_re.append(h_re)
                new_im.append(h_im)
            h16_ref[pl.ds(r0, store_rows), re0:re0 + SCAN_LANES] = (
                jnp.concatenate(new_re, axis=0).astype(BF16))
            h16_ref[pl.ds(r0, store_rows), im0:im0 + SCAN_LANES] = (
                jnp.concatenate(new_im, axis=0).astype(BF16))
            return h_re, h_im

        carry = (hc_ref[:, re0:re0 + SCAN_LANES], hc_ref[:, im0:im0 + SCAN_LANES])
        if t_chunk == steps_per_store:
            carry = scan_steps(0, carry)
        else:
            carry = lax.fori_loop(0, t_chunk // steps_per_store, scan_steps, carry,
                                  unroll=SCAN_UNROLL)
        hc_ref[:, re0:re0 + SCAN_LANES] = carry[0]
        hc_ref[:, im0:im0 + SCAN_LANES] = carry[1]
    h_re_out_ref[...] = hc_ref[:, 0:PART_LANES]
    h_im_out_ref[...] = hc_ref[:, PART_LANES:STATE_LANES]

    y_halves = []
    for hf in range(SSM_HALVES):
        l0 = hf * HALF_LANES
        y_halves.append(
            _dot_nt(h16_ref[:, l0:l0 + HALF_LANES], s5mat_ref[S5_C_RE, hf])
            + _dot_nt(h16_ref[:, PART_LANES + l0:PART_LANES + l0 + HALF_LANES],
                      s5mat_ref[S5_C_IM_NEG, hf]))
    s = jnp.concatenate(y_halves, axis=-1) + d_skip * ub_ref[...]

    gl = jax.nn.gelu(s).astype(BF16)
    branch_b = _dot(gl, w_glu_v_ref[...]) * _sigmoid(_dot(gl, w_glu_g_ref[...]))
    merged = (_sigmoid(gates_ref[:, 0:D_MODEL]) * ya_ref[...]
              + _sigmoid(gates_ref[:, D_MODEL:2 * D_MODEL]) * branch_b).astype(BF16)
    x1 = x + _dot(merged, w_out_ref[...])
    if stream_io:
        x1t_ref[slot] = x1.reshape(t_chunk, n_seq, D_MODEL)
        for c in store_x1(step, slot):
            c.start()

        @pl.when(step == last_step)
        def _drain_out():
            if t_chunk * 2 <= x_ref.shape[1]:
                for c in store_x1(step - 1, 1 - slot):
                    c.wait()
            for c in store_x1(step, slot):
                c.wait()
    else:
        x1_ref[...] = x1.reshape(x1_ref.shape)

    if cast_weights:
        @pl.when(step == pl.num_programs(0) - 1)
        def _finish_exports():
            for i in range(len(big_vmem)):
                export_copy(i).wait()

    for k, (_, _, _, n, _) in enumerate(later_jobs):
        @pl.when(step == k + 1)
        def _convert_chunk(k=k, n=n):
            if k >= 2:
                send(k - 2).wait()
            fetch(k).wait()
            stage16_ref[k % 2, 0:n] = stage_ref[k % 2, 0:n].astype(BF16)
            send(k).start()
            if k + 1 < len(later_jobs):
                fetch(k + 1).start()
    if later_jobs:
        @pl.when(step == len(later_jobs) + 1)
        def _finish_sends():
            for k in range(max(0, len(later_jobs) - 2), len(later_jobs)):
                send(k).wait()


def _mixer_call(x, pool_in, h0, small_w, big_w, later_w=(), *, n_seq, t_chunk, pos_start):
    rows = n_seq * t_chunk
    stream_io = t_chunk > 1
    has_state = pool_in is not None
    cast_weights = big_w[0].dtype == F32
    if stream_io:
        n_steps = x.shape[1] // t_chunk
        x1_shape, x1_spec = x.shape, _IN_HBM
    else:
        n_steps = 1
        x1_shape, x1_spec = (n_seq, D_MODEL), pl.BlockSpec((n_seq, D_MODEL), lambda i: (0, 0))
    keep_rows = POOL_BUF * n_seq

    in_arrays = [x]
    in_specs = [_IN_HBM]
    if has_state:
        in_arrays += [pool_in, *h0]
        in_specs += [_resident(a.shape) for a in (pool_in, *h0)]
    assert cast_weights or not later_w
    n_chunks = sum((w.shape[0] // min(CAST_ROWS, w.shape[0])) * (w.shape[1] // CAST_COLS)
                   for w in later_w)
    assert n_chunks + 2 <= n_steps or not later_w, "one chunk of later_w per grid step"
    in_arrays += list(small_w) + list(big_w) + list(later_w)
    in_specs += [_resident(w.shape) for w in small_w]
    in_specs += [_IN_HBM if cast_weights else _resident(w.shape) for w in big_w]
    in_specs += [_IN_HBM for _ in later_w]

    out_shape = [
        jax.ShapeDtypeStruct(x1_shape, F32),
        jax.ShapeDtypeStruct((keep_rows, POOL_WIDTH), F32),
        jax.ShapeDtypeStruct((n_seq, PART_LANES), F32),
        jax.ShapeDtypeStruct((n_seq, PART_LANES), F32),
    ]
    out_specs = [
        x1_spec,
        pl.BlockSpec((keep_rows, POOL_WIDTH), lambda i: (0, 0)),
        pl.BlockSpec((n_seq, PART_LANES), lambda i: (0, 0)),
        pl.BlockSpec((n_seq, PART_LANES), lambda i: (0, 0)),
    ]
    scratch = [
        pltpu.VMEM((rows, 2 * D_MODEL), F32),
        pltpu.VMEM((rows, SSM_WIDTH), F32),
        pltpu.VMEM(((POOL_SLOTS + t_chunk) * n_seq, POOL_WIDTH), F32),
        pltpu.VMEM((rows, STATE_LANES), F32),
        pltpu.VMEM((rows, STATE_LANES), BF16),
        pltpu.VMEM((n_seq, STATE_LANES), F32),
        pltpu.VMEM((rows, D_MODEL), F32),
        pltpu.VMEM((4, SSM_HALVES, HALF_IN, HALF_LANES), BF16),
    ]
    if cast_weights:
        out_shape += [jax.ShapeDtypeStruct(w.shape, BF16) for w in list(big_w) + list(later_w)]
        out_specs += [_IN_HBM for _ in list(big_w) + list(later_w)]
        scratch += [pltpu.VMEM(w.shape, BF16) for w in big_w]
        scratch += [pltpu.VMEM((2, CAST_ROWS, CAST_COLS), F32),
                    pltpu.VMEM((2, CAST_ROWS, CAST_COLS), BF16),
                    pltpu.SemaphoreType.DMA((2,)),
                    pltpu.SemaphoreType.DMA((len(big_w),)),
                    pltpu.SemaphoreType.DMA((2,))]
    if stream_io:
        scratch += [pltpu.VMEM((2, t_chunk, n_seq, D_MODEL), F32),
                    pltpu.VMEM((2, t_chunk, n_seq, D_MODEL), F32),
                    pltpu.SemaphoreType.DMA((2, n_seq)),
                    pltpu.SemaphoreType.DMA((2, n_seq))]
    else:
        scratch += [pltpu.VMEM((n_seq, D_MODEL), F32), pltpu.SemaphoreType.DMA((1,))]
    kern = functools.partial(_mixer_kernel, n_seq=n_seq, t_chunk=t_chunk, stream_io=stream_io,
                             pos_start=pos_start, has_state=has_state, cast_weights=cast_weights,
                             n_later=len(later_w))
    return pl.pallas_call(
        kern,
        grid=(n_steps,),
        in_specs=in_specs,
        out_specs=out_specs,
        out_shape=out_shape,
        scratch_shapes=scratch,
        compiler_params=pltpu.CompilerParams(
            dimension_semantics=("arbitrary",), vmem_limit_bytes=VMEM_LIMIT_BYTES),
        name=f"mixer_n{n_seq}_t{t_chunk}",
    )(*in_arrays)


def _ffn_rows(x1, p, g_ff_ref, g_ple_ref, g_final_ref, w_ff1_ref, w_ff2_ref, w_ple_ref,
              w_ple_gate_ref):
    h2 = _rmsnorm(x1, g_ff_ref[...]).astype(BF16)
    x2 = x1
    for c in range(D_FF // FF_CHUNK):
        hid = _dot(h2, w_ff1_ref[:, c * FF_CHUNK:(c + 1) * FF_CHUNK])
        hid = jnp.square(jnp.maximum(hid, 0.0)).astype(BF16)
        x2 = x2 + _dot(hid, w_ff2_ref[c * FF_CHUNK:(c + 1) * FF_CHUNK, :])
    h3 = _rmsnorm(x2, g_ple_ref[...]).astype(BF16)
    gate = _sigmoid(_dot(h3, w_ple_gate_ref[...]))
    x3 = x2 + _dot(p.astype(BF16), w_ple_ref[...]) * gate
    return _rmsnorm(x3, g_final_ref[...])


def _ffn_kernel(x1_ref, p_ref, x1s_ref, ps_hbm, *refs):
    params = refs[:7]
    out_ref, outs_hbm, ps_ref, ys_ref, sem = refs[7:]
    out_ref[...] = _ffn_rows(x1_ref[...], p_ref[...], *params)

    @pl.when(pl.program_id(0) == pl.num_programs(0) - 1)
    def _sample_rows():
        p_copy = pltpu.make_async_copy(ps_hbm.at[:, 0, :], ps_ref, sem.at[0])
        p_copy.start()
        p_copy.wait()
        ys_ref[...] = _ffn_rows(x1s_ref[...], ps_ref[...], *params)
        y_copy = pltpu.make_async_copy(ys_ref, outs_hbm.at[:, 0, :], sem.at[1])
        y_copy.start()
        y_copy.wait()


def _ffn_call(x1, p, x1s, ps, weights, *, block_rows):
    n_rows = x1.shape[0]
    n_s = x1s.shape[0]
    in_specs = [pl.BlockSpec((block_rows, D_MODEL), lambda i: (i, 0)),
                pl.BlockSpec((block_rows, PLE_DIM), lambda i: (i, 0)),
                _resident(x1s.shape), _IN_HBM]
    in_specs += [_resident(w.shape) for w in weights]
    return pl.pallas_call(
        _ffn_kernel,
        grid=(n_rows // block_rows,),
        in_specs=in_specs,
        out_specs=(pl.BlockSpec((block_rows, D_MODEL), lambda i: (i, 0)), _IN_HBM),
        out_shape=(jax.ShapeDtypeStruct((n_rows, D_MODEL), F32),
                   jax.ShapeDtypeStruct((n_s, 1, D_MODEL), F32)),
        scratch_shapes=[pltpu.VMEM((n_s, PLE_DIM), F32), pltpu.VMEM((n_s, D_MODEL), F32),
                        pltpu.SemaphoreType.DMA((2,))],
        compiler_params=pltpu.CompilerParams(
            dimension_semantics=("arbitrary",), vmem_limit_bytes=VMEM_LIMIT_BYTES),
        name="ffn",
    )(x1, p, x1s, ps, *weights)


def _s5_params(lam_re, lam_im, log_dt, b_re, b_im, c_re, c_im):
    dt = jnp.exp(log_dt)[:, None]
    mag = jnp.exp(lam_re * dt)
    ang = lam_im * dt
    abar_re = mag * jnp.cos(ang)
    abar_im = mag * jnp.sin(ang)
    den = lam_re * lam_re + lam_im * lam_im
    nr = abar_re - 1.0
    ni = abar_im
    k_re = ((nr * lam_re + ni * lam_im) / den)[:, None, :]
    k_im = ((ni * lam_re - nr * lam_im) / den)[:, None, :]
    b_re_t = jnp.swapaxes(b_re, 1, 2)
    b_im_t = jnp.swapaxes(b_im, 1, 2)
    bbar_re = k_re * b_re_t - k_im * b_im_t
    bbar_im = k_re * b_im_t + k_im * b_re_t
    abar = jnp.concatenate([abar_re.reshape(1, PART_LANES), abar_im.reshape(1, PART_LANES)],
                           axis=1)

    s5c = jnp.stack([bbar_re, bbar_im, c_re, c_im]).reshape(4 * SSM_WIDTH, SSM_STATE)
    return abar, s5c


def kernel(x_prompt, x_sample, p_prompt, p_sample, state_pool, state_ssm_re, state_ssm_im, g_mix, w_in, w_pool, pool_scale, lam_re, lam_im, log_dt, b_re, b_im, c_re, c_im, d_skip, w_glu_v, w_glu_g, w_out, g_ff, w_ff1, w_ff2, g_ple, w_ple, w_ple_gate, g_final):
    assert w_in.shape[0] == 1, "the final norm is fused into the (single) layer's ffn call"
    batch, seq_len, _ = x_prompt.shape
    dec_batch = x_sample.shape[0]

    abar, s5c = _s5_params(lam_re[0], lam_im[0], log_dt[0], b_re[0], b_im[0], c_re[0], c_im[0])
    small_w = (g_mix, pool_scale, d_skip, w_pool[0], abar, s5c)

    x1p, pool_p, re_p, im_p, *w16 = _mixer_call(
        x_prompt, None, None, small_w, (w_in[0], w_glu_v[0], w_glu_g[0], w_out[0]),
        (w_ff1[0], w_ff2[0], w_ple[0], w_ple_gate[0]),
        n_seq=batch, t_chunk=PROMPT_T_CHUNK, pos_start=0)
    mixer_w16, ffn_w16 = w16[:4], w16[4:]
    pool_tm = jnp.swapaxes(state_pool[0], 0, 1).reshape(POOL_BUF * dec_batch, POOL_WIDTH)
    h0 = (state_ssm_re.reshape(dec_batch, PART_LANES), state_ssm_im.reshape(dec_batch, PART_LANES))
    x1s, pool_s, re_s, im_s = _mixer_call(
        x_sample, pool_tm, h0, small_w, mixer_w16, n_seq=dec_batch, t_chunk=1, pos_start=PAST_LEN)

    y_prompt, y_sample = _ffn_call(
        x1p.reshape(batch * seq_len, D_MODEL), p_prompt.reshape(batch * seq_len, PLE_DIM),
        x1s, p_sample[0], [g_ff, g_ple, g_final.reshape(1, D_MODEL)] + ffn_w16,
        block_rows=FFN_BLOCK_ROWS)

    def to_buf(tm, n):
        return jnp.swapaxes(tm.reshape(POOL_BUF, n, POOL_WIDTH), 0, 1)[None]

    state_shape = lambda n: (1, n, SSM_GROUPS, SSM_STATE)
    return (y_prompt.reshape(batch, seq_len, D_MODEL), y_sample,
            to_buf(pool_p, batch), to_buf(pool_s, dec_batch),
            re_p.reshape(state_shape(batch)), im_p.reshape(state_shape(batch)),
            re_s.reshape(state_shape(dec_batch)), im_s.reshape(state_shape(dec_batch)))
```

```python
import functools

import jax
import jax.numpy as jnp
from jax import lax
from jax.experimental import pallas as pl
from jax.experimental.pallas import tpu as pltpu

D_MODEL = 1024
POOL_WINDOWS = (2, 4, 8, 16)
POOL_WIDTH = D_MODEL // 2
POOL_GROUP_DIM = POOL_WIDTH // len(POOL_WINDOWS)
POOL_OUT_GROUP_DIM = D_MODEL // len(POOL_WINDOWS)
POOL_BUF = max(POOL_WINDOWS) - 1
POOL_SLOTS = POOL_BUF + 1
SSM_WIDTH = D_MODEL // 2
SSM_GROUP_DIM = 16
SSM_GROUPS = SSM_WIDTH // SSM_GROUP_DIM
SSM_STATE = 64
PART_LANES = SSM_GROUPS * SSM_STATE
STATE_LANES = 2 * PART_LANES
SSM_HALVES = 2
HALF_IN = SSM_WIDTH // SSM_HALVES
HALF_GROUPS = SSM_GROUPS // SSM_HALVES
HALF_LANES = PART_LANES // SSM_HALVES
D_FF = 4 * D_MODEL
FF_CHUNK = 1024
PLE_DIM = 256
EPS = 1e-6
IN_WIDTH = POOL_WIDTH + SSM_WIDTH + 2 * D_MODEL
PAST_LEN = 16384
PROMPT_T_CHUNK = 64
FFN_BLOCK_ROWS = 1024
SCAN_LANES = 512
SCAN_UNROLL = 32
BF16_TILE_ROWS = 16
CAST_ROWS, CAST_COLS = 512, 1024
VMEM_LIMIT_BYTES = 58 * 1024 * 1024

S5_B_RE, S5_B_IM, S5_C_RE, S5_C_IM_NEG = range(4)

BF16 = jnp.bfloat16
F32 = jnp.float32


def _rmsnorm(x, g):
    return x * lax.rsqrt(jnp.mean(x * x, axis=-1, keepdims=True) + EPS) * g


def _sigmoid(x):
    return 0.5 * jnp.tanh(0.5 * x) + 0.5


def _dot(a, b):
    return jnp.dot(a, b, preferred_element_type=F32)


def _dot_nt(a, b):
    return lax.dot_general(a, b, (((1,), (1,)), ((), ())), preferred_element_type=F32)


def _resident(shape):
    nd = len(shape)
    return pl.BlockSpec(shape, lambda i, _nd=nd: (0,) * _nd, pipeline_mode=pl.Buffered(1))


_IN_HBM = pl.BlockSpec(memory_space=pl.ANY)


def _cast_jobs(pairs):
    jobs = []
    for src, dst in pairs:
        n_rows, n_cols = src.shape
        rows = min(CAST_ROWS, n_rows)
        jobs += [(src, dst, r0, rows, c0)
                 for r0 in range(0, n_rows, rows) for c0 in range(0, n_cols, CAST_COLS)]
    return jobs


def _stream_cast(pairs, stage_ref, sem, meanwhile=None):
    jobs = _cast_jobs(pairs)

    def copy(k):
        src, _, r0, rows, c0 = jobs[k]
        return pltpu.make_async_copy(src.at[r0:r0 + rows, c0:c0 + CAST_COLS],
                                     stage_ref.at[k % 2, 0:rows], sem.at[k % 2])

    for k in range(min(2, len(jobs))):
        copy(k).start()
    if meanwhile is not None:
        meanwhile()
    for k, (_, dst, r0, rows, c0) in enumerate(jobs):
        copy(k).wait()
        dst[r0:r0 + rows, c0:c0 + CAST_COLS] = stage_ref[k % 2, 0:rows].astype(BF16)
        if k + 2 < len(jobs):
            copy(k + 2).start()


def _mixer_kernel(*refs, n_seq, t_chunk, n_dec, n_later):
    it = iter(refs)
    take = lambda n: [next(it) for _ in range(n)]
    x_ref, xs_hbm, pool_s_in_hbm, h0_re_hbm, h0_im_hbm = take(5)
    g_mix_ref, pool_scale_ref, d_skip_ref, w_pool_ref, abar_ref, s5c_ref = take(6)
    big_in = take(4)
    later_in = take(n_later)
    x1_ref, pool_p_out_ref, re_p_out_ref, im_p_out_ref = take(4)
    x1s_out_ref, pool_s_out_hbm, re_s_out_ref, im_s_out_ref = take(4)
    later_out = take(n_later)
    prompt_bufs = take(7)
    sample_bufs = take(7)
    (s5mat_ref,) = take(1)
    w_in_ref, w_glu_v_ref, w_glu_g_ref, w_out_ref = big_vmem = take(4)
    stage_ref, stage16_ref, cast_sem, send_sem = take(4)
    xt_ref, x1t_ref, in_sem, out_sem = take(4)
    xs_ref, dec_sem = take(2)

    rows = n_seq * t_chunk
    step = pl.program_id(0)
    last_step = pl.num_programs(0) - 1
    g_mix = g_mix_ref[...]
    pool_scale = pool_scale_ref[...]
    d_skip = d_skip_ref[...]

    def load_x(i, slot):
        return [pltpu.make_async_copy(x_ref.at[b, pl.ds(i * t_chunk, t_chunk), :],
                                      xt_ref.at[slot, :, b, :], in_sem.at[slot, b])
                for b in range(n_seq)]

    def store_x1(i, slot):
        return [pltpu.make_async_copy(x1t_ref.at[slot, :, b, :],
                                      x1_ref.at[b, pl.ds(i * t_chunk, t_chunk), :],
                                      out_sem.at[slot, b])
                for b in range(n_seq)]

    later_jobs = _cast_jobs(list(zip(later_in, later_out)))

    def fetch(k):
        src, _, r0, n, c0 = later_jobs[k]
        return pltpu.make_async_copy(src.at[r0:r0 + n, c0:c0 + CAST_COLS],
                                     stage_ref.at[k % 2, 0:n], cast_sem.at[k % 2])

    def send(k):
        _, dst, r0, n, c0 = later_jobs[k]
        return pltpu.make_async_copy(stage16_ref.at[k % 2, 0:n],
                                     dst.at[r0:r0 + n, c0:c0 + CAST_COLS], send_sem.at[k % 2])

    def build_s5mat():
        lane = lax.broadcasted_iota(jnp.int32, (SSM_STATE, HALF_LANES), 1)
        tile16 = (lax.broadcasted_iota(jnp.int32, (SSM_STATE, HALF_LANES), 0)
                  == jnp.bitwise_and(lane, SSM_STATE - 1)).astype(BF16)
        row_g = lax.shift_right_logical(
            lax.broadcasted_iota(jnp.int32, (HALF_IN, HALF_LANES), 0), SSM_GROUP_DIM.bit_length() - 1)
        col_g = lax.shift_right_logical(
            lax.broadcasted_iota(jnp.int32, (HALF_IN, HALF_LANES), 1), SSM_STATE.bit_length() - 1)
        for which in range(4):
            for hf in range(SSM_HALVES):
                r0 = (which * SSM_HALVES + hf) * HALF_IN
                blk = _dot(s5c_ref[r0:r0 + HALF_IN, :].astype(BF16), tile16)
                blk = jnp.where(row_g == col_g, blk, 0.0)
                s5mat_ref[which, hf] = (-blk if which == S5_C_IM_NEG else blk).astype(BF16)

    def mix(x, n, t_len, pos0, bufs, h_re_out_ref, h_im_out_ref):
        gates_ref, ub_ref, ext_ref, hs_ref, h16_ref, hc_ref, ya_ref = bufs
        n_rows = n * t_len
        cur0 = POOL_SLOTS * n
        h = _rmsnorm(x, g_mix).astype(BF16)
        g0 = POOL_WIDTH + SSM_WIDTH
        u = _dot(h, w_in_ref[:, 0:g0])
        ext_ref[cur0:cur0 + n_rows, :] = u[:, 0:POOL_WIDTH]
        ub_ref[...] = u[:, POOL_WIDTH:g0]
        u_b16 = u[:, POOL_WIDTH:g0].astype(BF16)
        for hf in range(SSM_HALVES):
            u_half = u_b16[:, hf * HALF_IN:(hf + 1) * HALF_IN]
            l0 = hf * HALF_LANES
            hs_ref[:, l0:l0 + HALF_LANES] = _dot(u_half, s5mat_ref[S5_B_RE, hf])
            hs_ref[:, PART_LANES + l0:PART_LANES + l0 + HALF_LANES] = _dot(
                u_half, s5mat_ref[S5_B_IM, hf])
        gates_ref[...] = _dot(h, w_in_ref[:, g0:IN_WIDTH])

        t_idx = lax.shift_right_logical(
            lax.broadcasted_iota(jnp.int32, (n_rows, POOL_GROUP_DIM), 0), n.bit_length() - 1)
        pos = t_idx + pos0
        for gi, w in enumerate(POOL_WINDOWS):
            c0 = gi * POOL_GROUP_DIM
            cur = ext_ref[cur0:cur0 + n_rows, c0:c0 + POOL_GROUP_DIM]
            acc = cur
            for k in range(1, w):
                r0 = cur0 - k * n
                acc = acc + ext_ref[r0:r0 + n_rows, c0:c0 + POOL_GROUP_DIM]
            count = jnp.minimum(pos + 1, w).astype(F32)
            pooled = (acc / count - cur).astype(BF16)
            o0 = gi * POOL_OUT_GROUP_DIM
            ya_ref[:, o0:o0 + POOL_OUT_GROUP_DIM] = (
                _dot(pooled, w_pool_ref[gi].astype(BF16))
                * pool_scale[:, o0:o0 + POOL_OUT_GROUP_DIM])

        steps_per_store = max(1, BF16_TILE_ROWS // n)
        store_rows = steps_per_store * n
        for q in range(PART_LANES // SCAN_LANES):
            re0 = q * SCAN_LANES
            im0 = PART_LANES + re0
            a_re = abar_ref[:, re0:re0 + SCAN_LANES]
            a_im = abar_ref[:, im0:im0 + SCAN_LANES]
            if t_len > 1:
                a_re = jnp.broadcast_to(a_re, (n, SCAN_LANES))
                a_im = jnp.broadcast_to(a_im, (n, SCAN_LANES))

            def scan_steps(i, carry, re0=re0, im0=im0, a_re=a_re, a_im=a_im):
                h_re, h_im = carry
                r0 = i * store_rows if isinstance(i, int) else pl.multiple_of(i * store_rows,
                                                                              store_rows)
                new_re, new_im = [], []
                for s in range(steps_per_store):
                    rs = r0 + s * n
                    h_re, h_im = (
                        a_re * h_re - a_im * h_im + hs_ref[pl.ds(rs, n), re0:re0 + SCAN_LANES],
                        a_re * h_im + a_im * h_re + hs_ref[pl.ds(rs, n), im0:im0 + SCAN_LANES])
                    new_re.append(h_re)
                    new_im.append(h_im)
                h16_ref[pl.ds(r0, store_rows), re0:re0 + SCAN_LANES] = (
                    jnp.concatenate(new_re, axis=0).astype(BF16))
                h16_ref[pl.ds(r0, store_rows), im0:im0 + SCAN_LANES] = (
                    jnp.concatenate(new_im, axis=0).astype(BF16))
                return h_re, h_im

            carry = (hc_ref[:, re0:re0 + SCAN_LANES], hc_ref[:, im0:im0 + SCAN_LANES])
            if t_len == steps_per_store:
                carry = scan_steps(0, carry)
            else:
                carry = lax.fori_loop(0, t_len // steps_per_store, scan_steps, carry,
                                      unroll=SCAN_UNROLL)
            hc_ref[:, re0:re0 + SCAN_LANES] = carry[0]
            hc_ref[:, im0:im0 + SCAN_LANES] = carry[1]
        h_re_out_ref[...] = hc_ref[:, 0:PART_LANES]
        h_im_out_ref[...] = hc_ref[:, PART_LANES:STATE_LANES]

        y_halves = []
        for hf in range(SSM_HALVES):
            l0 = hf * HALF_LANES
            y_halves.append(
                _dot_nt(h16_ref[:, l0:l0 + HALF_LANES], s5mat_ref[S5_C_RE, hf])
                + _dot_nt(h16_ref[:, PART_LANES + l0:PART_LANES + l0 + HALF_LANES],
                          s5mat_ref[S5_C_IM_NEG, hf]))
        s = jnp.concatenate(y_halves, axis=-1) + d_skip * ub_ref[...]

        gl = jax.nn.gelu(s).astype(BF16)
        branch_b = _dot(gl, w_glu_v_ref[...]) * _sigmoid(_dot(gl, w_glu_g_ref[...]))
        merged = (_sigmoid(gates_ref[:, 0:D_MODEL]) * ya_ref[...]
                  + _sigmoid(gates_ref[:, D_MODEL:2 * D_MODEL]) * branch_b).astype(BF16)
        return x + _dot(merged, w_out_ref[...])

    ext_ref, hc_ref = prompt_bufs[2], prompt_bufs[5]
    cur0 = POOL_SLOTS * n_seq

    @pl.when(step == 0)
    def _init():
        for c in load_x(0, 0):
            c.start()
        ext_ref[0:cur0, :] = jnp.zeros((cur0, POOL_WIDTH), F32)
        hc_ref[...] = jnp.zeros((n_seq, STATE_LANES), F32)
        _stream_cast(list(zip(big_in, big_vmem)), stage_ref, cast_sem, meanwhile=build_s5mat)
        if later_jobs:
            fetch(0).start()

    slot = lax.rem(step, 2)

    @pl.when(step < last_step)
    def _prefetch():
        for c in load_x(step + 1, 1 - slot):
            c.start()

    for c in load_x(step, slot):
        c.wait()

    @pl.when(step >= 2)
    def _free_out_slot():
        for c in store_x1(step - 2, slot):
            c.wait()

    x1 = mix(xt_ref[slot].reshape(rows, D_MODEL), n_seq, t_chunk, step * t_chunk, prompt_bufs,
             re_p_out_ref, im_p_out_ref)
    x1t_ref[slot] = x1.reshape(t_chunk, n_seq, D_MODEL)
    for c in store_x1(step, slot):
        c.start()
    pool_p_out_ref[...] = ext_ref[cur0 + rows - POOL_BUF * n_seq:cur0 + rows, :]
    ext_ref[0:cur0, :] = ext_ref[rows:rows + cur0, :]

    for k, (_, _, _, n, _) in enumerate(later_jobs):
        @pl.when(step == k + 1)
        def _convert_chunk(k=k, n=n):
            if k >= 2:
                send(k - 2).wait()
            fetch(k).wait()
            stage16_ref[k % 2, 0:n] = stage_ref[k % 2, 0:n].astype(BF16)
            send(k).start()
            if k + 1 < len(later_jobs):
                fetch(k + 1).start()
    if later_jobs:
        @pl.when(step == len(later_jobs) + 1)
        def _finish_sends():
            for k in range(max(0, len(later_jobs) - 2), len(later_jobs)):
                send(k).wait()

    @pl.when(step == last_step)
    def _sample_and_drain():
        ext_s_ref, hc_s_ref = sample_bufs[2], sample_bufs[5]
        first = POOL_SLOTS * n_dec
        loads = [
            pltpu.make_async_copy(xs_hbm.at[:, 0, :], xs_ref, dec_sem.at[0]),
            pltpu.make_async_copy(pool_s_in_hbm, ext_s_ref.at[n_dec:first, :], dec_sem.at[1]),
            pltpu.make_async_copy(h0_re_hbm, hc_s_ref.at[:, 0:PART_LANES], dec_sem.at[2]),
            pltpu.make_async_copy(h0_im_hbm, hc_s_ref.at[:, PART_LANES:STATE_LANES],
                                  dec_sem.at[3]),
        ]
        for c in loads:
            c.start()
        for c in loads:
            c.wait()
        x1s_out_ref[...] = mix(xs_ref[...], n_dec, 1, PAST_LEN, sample_bufs,
                               re_s_out_ref, im_s_out_ref)
        pool_copy = pltpu.make_async_copy(
            ext_s_ref.at[first + n_dec - POOL_BUF * n_dec:first + n_dec, :], pool_s_out_hbm,
            dec_sem.at[4])
        pool_copy.start()
        pool_copy.wait()
        if t_chunk * 2 <= x_ref.shape[1]:
            for c in store_x1(step - 1, 1 - slot):
                c.wait()
        for c in store_x1(step, slot):
            c.wait()


def _mixer_call(x, x_dec, pool_dec, h0_dec, small_w, big_w, later_w, *, t_chunk):
    n_seq, seq_len, _ = x.shape
    n_dec = x_dec.shape[0]
    n_steps = seq_len // t_chunk
    rows = n_seq * t_chunk
    n_chunks = sum((w.shape[0] // min(CAST_ROWS, w.shape[0])) * (w.shape[1] // CAST_COLS)
                   for w in later_w)
    assert n_chunks + 2 <= n_steps, "one chunk of later_w per grid step"
    assert t_chunk >= POOL_SLOTS

    in_arrays = [x, x_dec, pool_dec, *h0_dec, *small_w, *big_w, *later_w]
    in_specs = ([_IN_HBM] * 5 + [_resident(w.shape) for w in small_w]
                + [_IN_HBM] * (len(big_w) + len(later_w)))

    def resident_out(shape):
        return jax.ShapeDtypeStruct(shape, F32), pl.BlockSpec(shape, lambda i: (0,) * len(shape))

    hbm_out = lambda shape, dtype=F32: (jax.ShapeDtypeStruct(shape, dtype), _IN_HBM)
    outs = [hbm_out(x.shape), resident_out((POOL_BUF * n_seq, POOL_WIDTH)),
            resident_out((n_seq, PART_LANES)), resident_out((n_seq, PART_LANES)),
            resident_out((n_dec, D_MODEL)), hbm_out((POOL_BUF * n_dec, POOL_WIDTH)),
            resident_out((n_dec, PART_LANES)), resident_out((n_dec, PART_LANES))]
    outs += [hbm_out(w.shape, BF16) for w in later_w]

    def chunk_bufs(n, t_len):
        r = n * t_len
        return [
            pltpu.VMEM((r, 2 * D_MODEL), F32),
            pltpu.VMEM((r, SSM_WIDTH), F32),
            pltpu.VMEM(((POOL_SLOTS + t_len) * n, POOL_WIDTH), F32),
            pltpu.VMEM((r, STATE_LANES), F32),
            pltpu.VMEM((r, STATE_LANES), BF16),
            pltpu.VMEM((n, STATE_LANES), F32),
            pltpu.VMEM((r, D_MODEL), F32),
        ]

    scratch = chunk_bufs(n_seq, t_chunk) + chunk_bufs(n_dec, 1)
    scratch += [pltpu.VMEM((4, SSM_HALVES, HALF_IN, HALF_LANES), BF16)]
    scratch += [pltpu.VMEM(w.shape, BF16) for w in big_w]
    scratch += [pltpu.VMEM((2, CAST_ROWS, CAST_COLS), F32),
                pltpu.VMEM((2, CAST_ROWS, CAST_COLS), BF16),
                pltpu.SemaphoreType.DMA((2,)),
                pltpu.SemaphoreType.DMA((2,))]
    scratch += [pltpu.VMEM((2, t_chunk, n_seq, D_MODEL), F32),
                pltpu.VMEM((2, t_chunk, n_seq, D_MODEL), F32),
                pltpu.SemaphoreType.DMA((2, n_seq)),
                pltpu.SemaphoreType.DMA((2, n_seq))]
    scratch += [pltpu.VMEM((n_dec, D_MODEL), F32), pltpu.SemaphoreType.DMA((5,))]
    kern = functools.partial(_mixer_kernel, n_seq=n_seq, t_chunk=t_chunk, n_dec=n_dec,
                             n_later=len(later_w))
    return pl.pallas_call(
        kern,
        grid=(n_steps,),
        in_specs=in_specs,
        out_specs=[spec for _, spec in outs],
        out_shape=[shape for shape, _ in outs],
        scratch_shapes=scratch,
        compiler_params=pltpu.CompilerParams(
            dimension_semantics=("arbitrary",), vmem_limit_bytes=VMEM_LIMIT_BYTES),
        name="mixer",
    )(*in_arrays)


def _ffn_rows(x1, p, g_ff_ref, g_ple_ref, g_final_ref, w_ff1_ref, w_ff2_ref, w_ple_ref,
              w_ple_gate_ref):
    h2 = _rmsnorm(x1, g_ff_ref[...]).astype(BF16)
    x2 = x1
    for c in range(D_FF // FF_CHUNK):
        hid = _dot(h2, w_ff1_ref[:, c * FF_CHUNK:(c + 1) * FF_CHUNK])
        hid = jnp.square(jnp.maximum(hid, 0.0)).astype(BF16)
        x2 = x2 + _dot(hid, w_ff2_ref[c * FF_CHUNK:(c + 1) * FF_CHUNK, :])
    h3 = _rmsnorm(x2, g_ple_ref[...]).astype(BF16)
    gate = _sigmoid(_dot(h3, w_ple_gate_ref[...]))
    x3 = x2 + _dot(p.astype(BF16), w_ple_ref[...]) * gate
    return _rmsnorm(x3, g_final_ref[...])


def _ffn_kernel(x1_ref, p_ref, x1s_ref, ps_hbm, *refs):
    params = refs[:7]
    out_ref, outs_hbm, ps_ref, ys_ref, sem = refs[7:]
    out_ref[...] = _ffn_rows(x1_ref[...], p_ref[...], *params)

    @pl.when(pl.program_id(0) == pl.num_programs(0) - 1)
    def _sample_rows():
        p_copy = pltpu.make_async_copy(ps_hbm.at[:, 0, :], ps_ref, sem.at[0])
        p_copy.start()
        p_copy.wait()
        ys_ref[...] = _ffn_rows(x1s_ref[...], ps_ref[...], *params)
        y_copy = pltpu.make_async_copy(ys_ref, outs_hbm.at[:, 0, :], sem.at[1])
        y_copy.start()
        y_copy.wait()


def _ffn_call(x1, p, x1s, ps, weights, *, block_rows):
    n_rows = x1.shape[0]
    n_s = x1s.shape[0]
    in_specs = [pl.BlockSpec((block_rows, D_MODEL), lambda i: (i, 0)),
                pl.BlockSpec((block_rows, PLE_DIM), lambda i: (i, 0)),
                _resident(x1s.shape), _IN_HBM]
    in_specs += [_resident(w.shape) for w in weights]
    return pl.pallas_call(
        _ffn_kernel,
        grid=(n_rows // block_rows,),
        in_specs=in_specs,
        out_specs=(pl.BlockSpec((block_rows, D_MODEL), lambda i: (i, 0)), _IN_HBM),
        out_shape=(jax.ShapeDtypeStruct((n_rows, D_MODEL), F32),
                   jax.ShapeDtypeStruct((n_s, 1, D_MODEL), F32)),
        scratch_shapes=[pltpu.VMEM((n_s, PLE_DIM), F32), pltpu.VMEM((n_s, D_MODEL), F32),
                        pltpu.SemaphoreType.DMA((2,))],
        input_output_aliases={0: 0},
        compiler_params=pltpu.CompilerParams(
            dimension_semantics=("arbitrary",), vmem_limit_bytes=VMEM_LIMIT_BYTES),
        name="ffn",
    )(x1, p, x1s, ps, *weights)


def _s5_params(lam_re, lam_im, log_dt, b_re, b_im, c_re, c_im):
    dt = jnp.exp(log_dt)[:, None]
    mag = jnp.exp(lam_re * dt)
    ang = lam_im * dt
    abar_re = mag * jnp.cos(ang)
    abar_im = mag * jnp.sin(ang)
    den = lam_re * lam_re + lam_im * lam_im
    nr = abar_re - 1.0
    ni = abar_im
    k_re = ((nr * lam_re + ni * lam_im) / den)[:, None, :]
    k_im = ((ni * lam_re - nr * lam_im) / den)[:, None, :]
    b_re_t = jnp.swapaxes(b_re, 1, 2)
    b_im_t = jnp.swapaxes(b_im, 1, 2)
    bbar_re = k_re * b_re_t - k_im * b_im_t
    bbar_im = k_re * b_im_t + k_im * b_re_t
    abar = jnp.concatenate([abar_re.reshape(1, PART_LANES), abar_im.reshape(1, PART_LANES)],
                           axis=1)
    s5c = jnp.stack([bbar_re, bbar_im, c_re, c_im]).reshape(4 * SSM_WIDTH, SSM_STATE)
    return abar, s5c


def kernel(x_prompt, x_sample, p_prompt, p_sample, state_pool, state_ssm_re, state_ssm_im, g_mix, w_in, w_pool, pool_scale, lam_re, lam_im, log_dt, b_re, b_im, c_re, c_im, d_skip, w_glu_v, w_glu_g, w_out, g_ff, w_ff1, w_ff2, g_ple, w_ple, w_ple_gate, g_final):
    assert w_in.shape[0] == 1, "the final norm is fused into the (single) layer's ffn call"
    batch, seq_len, _ = x_prompt.shape
    dec_batch = x_sample.shape[0]

    abar, s5c = _s5_params(lam_re[0], lam_im[0], log_dt[0], b_re[0], b_im[0], c_re[0], c_im[0])
    small_w = (g_mix, pool_scale, d_skip, w_pool[0], abar, s5c)
    pool_tm = jnp.swapaxes(state_pool[0], 0, 1).reshape(POOL_BUF * dec_batch, POOL_WIDTH)
    h0 = (state_ssm_re.reshape(dec_batch, PART_LANES), state_ssm_im.reshape(dec_batch, PART_LANES))

    x1p, pool_p, re_p, im_p, x1s, pool_s, re_s, im_s, *ffn_w16 = _mixer_call(
        x_prompt, x_sample, pool_tm, h0, small_w, (w_in[0], w_glu_v[0], w_glu_g[0], w_out[0]),
        (w_ff1[0], w_ff2[0], w_ple[0], w_ple_gate[0]), t_chunk=PROMPT_T_CHUNK)

    y_prompt, y_sample = _ffn_call(
        x1p.reshape(batch * seq_len, D_MODEL), p_prompt.reshape(batch * seq_len, PLE_DIM),
        x1s, p_sample[0], [g_ff, g_ple, g_final.reshape(1, D_MODEL)] + ffn_w16,
        block_rows=FFN_BLOCK_ROWS)

    def to_buf(tm, n):
        return jnp.swapaxes(tm.reshape(POOL_BUF, n, POOL_WIDTH), 0, 1)[None]

    state_shape = lambda n: (1, n, SSM_GROUPS, SSM_STATE)
    return (y_prompt.reshape(batch, seq_len, D_MODEL), y_sample,
            to_buf(pool_p, batch), to_buf(pool_s, dec_batch),
            re_p.reshape(state_shape(batch)), im_p.reshape(state_shape(batch)),
            re_s.reshape(state_shape(dec_batch)), im_s.reshape(state_shape(dec_batch)))
```

```python
import functools

import jax
import jax.numpy as jnp
from jax import lax
from jax.experimental import pallas as pl
from jax.experimental.pallas import tpu as pltpu

D_MODEL = 1024
POOL_WINDOWS = (2, 4, 8, 16)
POOL_WIDTH = D_MODEL // 2
POOL_GROUP_DIM = POOL_WIDTH // len(POOL_WINDOWS)
POOL_OUT_GROUP_DIM = D_MODEL // len(POOL_WINDOWS)
POOL_BUF = max(POOL_WINDOWS) - 1
POOL_SLOTS = POOL_BUF + 1
SSM_WIDTH = D_MODEL // 2
SSM_GROUP_DIM = 16
SSM_GROUPS = SSM_WIDTH // SSM_GROUP_DIM
SSM_STATE = 64
PART_LANES = SSM_GROUPS * SSM_STATE
STATE_LANES = 2 * PART_LANES
SSM_HALVES = 2
HALF_IN = SSM_WIDTH // SSM_HALVES
HALF_GROUPS = SSM_GROUPS // SSM_HALVES
HALF_LANES = PART_LANES // SSM_HALVES
D_FF = 4 * D_MODEL
FF_CHUNK = 1024
PLE_DIM = 256
EPS = 1e-6
IN_WIDTH = POOL_WIDTH + SSM_WIDTH + 2 * D_MODEL
PAST_LEN = 16384
PROMPT_T_CHUNK = 64
FFN_BLOCK_ROWS = 1024
SCAN_LANES = 512
SCAN_UNROLL = 32
BF16_TILE_ROWS = 16
CAST_ROWS, CAST_COLS = 512, 1024
VMEM_LIMIT_BYTES = 58 * 1024 * 1024

S5_B_RE, S5_B_IM, S5_C_RE, S5_C_IM_NEG = range(4)

BF16 = jnp.bfloat16
F32 = jnp.float32


def _rmsnorm(x, g):
    return x * lax.rsqrt(jnp.mean(x * x, axis=-1, keepdims=True) + EPS) * g


def _sigmoid(x):
    return 0.5 * jnp.tanh(0.5 * x) + 0.5


def _dot(a, b):
    return jnp.dot(a, b, preferred_element_type=F32)


def _dot_nt(a, b):
    return lax.dot_general(a, b, (((1,), (1,)), ((), ())), preferred_element_type=F32)


def _resident(shape):
    nd = len(shape)
    return pl.BlockSpec(shape, lambda i, _nd=nd: (0,) * _nd, pipeline_mode=pl.Buffered(1))


_IN_HBM = pl.BlockSpec(memory_space=pl.ANY)


def _cast_jobs(pairs):
    jobs = []
    for src, dst in pairs:
        n_rows, n_cols = src.shape
        rows = min(CAST_ROWS, n_rows)
        jobs += [(src, dst, r0, rows, c0)
                 for r0 in range(0, n_rows, rows) for c0 in range(0, n_cols, CAST_COLS)]
    return jobs


def _stream_cast(pairs, stage_ref, sem):
    jobs = _cast_jobs(pairs)

    def copy(k):
        src, _, r0, rows, c0 = jobs[k]
        return pltpu.make_async_copy(src.at[r0:r0 + rows, c0:c0 + CAST_COLS],
                                     stage_ref.at[k % 2, 0:rows], sem.at[k % 2])

    copy(0).start()
    for k, (_, dst, r0, rows, c0) in enumerate(jobs):
        if k + 1 < len(jobs):
            copy(k + 1).start()
        copy(k).wait()
        dst[r0:r0 + rows, c0:c0 + CAST_COLS] = stage_ref[k % 2, 0:rows].astype(BF16)


def _mixer_kernel(*refs, n_seq, t_chunk, stream_io, pos_start, has_state, cast_weights, n_later):
    it = iter(refs)
    x_ref = next(it)
    pool_in_ref = next(it) if has_state else None
    h0_ref = next(it) if has_state else None
    g_mix_ref, pool_scale_ref, d_skip_ref, w_pool_ref, abar_ref, s5mat_ref = (
        next(it) for _ in range(6))
    big_in = [next(it) for _ in range(4)]
    later_in = [next(it) for _ in range(n_later)]
    x1_ref, ua_out_ref, h_re_out_ref, h_im_out_ref = (next(it) for _ in range(4))
    big_out = [next(it) for _ in range(4)] if cast_weights else None
    later_out = [next(it) for _ in range(n_later)]
    gates_ref, ub_ref, ext_ref, hs_ref, h16_ref, hc_ref, ya_ref = (next(it) for _ in range(7))
    if cast_weights:
        big_vmem = [next(it) for _ in range(4)]
        stage_ref, stage16_ref, cast_sem, export_sem, send_sem = (next(it) for _ in range(5))
    else:
        big_vmem = big_in
    if stream_io:
        xt_ref, x1t_ref, in_sem, out_sem = (next(it) for _ in range(4))
    else:
        xs_ref, xs_sem = (next(it) for _ in range(2))
    w_in_ref, w_glu_v_ref, w_glu_g_ref, w_out_ref = big_vmem

    rows = n_seq * t_chunk
    step = pl.program_id(0)
    last_step = pl.num_programs(0) - 1

    def load_x(i, slot):
        return [pltpu.make_async_copy(x_ref.at[b, pl.ds(i * t_chunk, t_chunk), :],
                                      xt_ref.at[slot, :, b, :], in_sem.at[slot, b])
                for b in range(n_seq)]

    def store_x1(i, slot):
        return [pltpu.make_async_copy(x1t_ref.at[slot, :, b, :],
                                      x1_ref.at[b, pl.ds(i * t_chunk, t_chunk), :],
                                      out_sem.at[slot, b])
                for b in range(n_seq)]
    cur0 = POOL_SLOTS * n_seq
    g_mix = g_mix_ref[...]
    pool_scale = pool_scale_ref[...]
    d_skip = d_skip_ref[...]

    def export_copy(i):
        return pltpu.make_async_copy(big_vmem[i], big_out[i], export_sem.at[i])

    later_jobs = _cast_jobs(list(zip(later_in, later_out)))

    def fetch(k):
        src, _, r0, n, c0 = later_jobs[k]
        return pltpu.make_async_copy(src.at[r0:r0 + n, c0:c0 + CAST_COLS],
                                     stage_ref.at[k % 2, 0:n], cast_sem.at[k % 2])

    def send(k):
        _, dst, r0, n, c0 = later_jobs[k]
        return pltpu.make_async_copy(stage16_ref.at[k % 2, 0:n],
                                     dst.at[r0:r0 + n, c0:c0 + CAST_COLS], send_sem.at[k % 2])

    @pl.when(step == 0)
    def _init():
        if stream_io:
            for c in load_x(0, 0):
                c.start()
        if has_state:
            ext_ref[n_seq:cur0, :] = pool_in_ref[...]
            hc_ref[...] = h0_ref[...]
        else:
            ext_ref[0:cur0, :] = jnp.zeros((cur0, POOL_WIDTH), F32)
            hc_ref[...] = jnp.zeros((n_seq, STATE_LANES), F32)
        if cast_weights:
            _stream_cast(list(zip(big_in, big_vmem)), stage_ref, cast_sem)
            for i in range(len(big_vmem)):
                export_copy(i).start()
            if later_jobs:
                fetch(0).start()

    if stream_io:
        slot = lax.rem(step, 2)

        @pl.when(step < last_step)
        def _prefetch():
            for c in load_x(step + 1, 1 - slot):
                c.start()

        for c in load_x(step, slot):
            c.wait()

        @pl.when(step >= 2)
        def _free_out_slot():
            for c in store_x1(step - 2, slot):
                c.wait()

        x = xt_ref[slot].reshape(rows, D_MODEL)
    else:
        x_copy = pltpu.make_async_copy(x_ref.at[:, 0, :], xs_ref, xs_sem.at[0])
        x_copy.start()
        x_copy.wait()
        x = xs_ref[...]
    h = _rmsnorm(x, g_mix).astype(BF16)
    g0 = POOL_WIDTH + SSM_WIDTH
    u = _dot(h, w_in_ref[:, 0:g0])
    ext_ref[cur0:cur0 + rows, :] = u[:, 0:POOL_WIDTH]
    ub_ref[...] = u[:, POOL_WIDTH:g0]
    u_b16 = u[:, POOL_WIDTH:g0].astype(BF16)
    for hf in range(SSM_HALVES):
        u_half = u_b16[:, hf * HALF_IN:(hf + 1) * HALF_IN]
        l0 = hf * HALF_LANES
        hs_ref[:, l0:l0 + HALF_LANES] = _dot(u_half, s5mat_ref[S5_B_RE, hf])
        hs_ref[:, PART_LANES + l0:PART_LANES + l0 + HALF_LANES] = _dot(
            u_half, s5mat_ref[S5_B_IM, hf])
    gates_ref[...] = _dot(h, w_in_ref[:, g0:IN_WIDTH])

    t_idx = lax.shift_right_logical(
        lax.broadcasted_iota(jnp.int32, (rows, POOL_GROUP_DIM), 0), n_seq.bit_length() - 1)
    pos = t_idx + (pos_start + step * t_chunk)
    for gi, w in enumerate(POOL_WINDOWS):
        c0 = gi * POOL_GROUP_DIM
        cur = ext_ref[cur0:cur0 + rows, c0:c0 + POOL_GROUP_DIM]
        acc = cur
        for k in range(1, w):
            r0 = cur0 - k * n_seq
            acc = acc + ext_ref[r0:r0 + rows, c0:c0 + POOL_GROUP_DIM]
        count = jnp.minimum(pos + 1, w).astype(F32)
        pooled = (acc / count - cur).astype(BF16)
        o0 = gi * POOL_OUT_GROUP_DIM
        ya_ref[:, o0:o0 + POOL_OUT_GROUP_DIM] = (
            _dot(pooled, w_pool_ref[gi].astype(BF16))
            * pool_scale[:, o0:o0 + POOL_OUT_GROUP_DIM])
    ua_out_ref[...] = ext_ref[cur0 + rows - POOL_BUF * n_seq:cur0 + rows, :]
    if t_chunk >= POOL_SLOTS:
        ext_ref[0:cur0, :] = ext_ref[rows:rows + cur0, :]

    steps_per_store = max(1, BF16_TILE_ROWS // n_seq)
    store_rows = steps_per_store * n_seq
    for q in range(PART_LANES // SCAN_LANES):
        re0 = q * SCAN_LANES
        im0 = PART_LANES + re0
        a_re = abar_ref[:, re0:re0 + SCAN_LANES]
        a_im = abar_ref[:, im0:im0 + SCAN_LANES]
        if t_chunk > 1:
            a_re = jnp.broadcast_to(a_re, (n_seq, SCAN_LANES))
            a_im = jnp.broadcast_to(a_im, (n_seq, SCAN_LANES))

        def scan_steps(i, carry, re0=re0, im0=im0, a_re=a_re, a_im=a_im):
            h_re, h_im = carry
            r0 = i * store_rows if isinstance(i, int) else pl.multiple_of(i * store_rows,
                                                                          store_rows)
            new_re, new_im = [], []
            for s in range(steps_per_store):
                rs = r0 + s * n_seq
                h_re, h_im = (
                    a_re * h_re - a_im * h_im + hs_ref[pl.ds(rs, n_seq), re0:re0 + SCAN_LANES],
                    a_re * h_im + a_im * h_re + hs_ref[pl.ds(rs, n_seq), im0:im0 + SCAN_LANES])
                new_re.append(h_re)
                new_im.append(h_im)
            h16_ref[pl.ds(r0, store_rows), re0:re0 + SCAN_LANES] = (
                jnp.concatenate(new_re, axis=0).astype(BF16))
            h16_ref[pl.ds(r0, store_rows), im0:im0 + SCAN_LANES] = (
                jnp.concatenate(new_im, axis=0).astype(BF16))
            return h_re, h_im

        carry = (hc_ref[:, re0:re0 + SCAN_LANES], hc_ref[:, im0:im0 + SCAN_LANES])
        if t_chunk == steps_per_store:
            carry = scan_steps(0, carry)
        else:
            carry = lax.fori_loop(0, t_chunk // steps_per_store, scan_steps, carry,
                                  unroll=SCAN_UNROLL)
        hc_ref[:, re0:re0 + SCAN_LANES] = carry[0]
        hc_ref[:, im0:im0 + SCAN_LANES] = carry[1]
    h_re_out_ref[...] = hc_ref[:, 0:PART_LANES]
    h_im_out_ref[...] = hc_ref[:, PART_LANES:STATE_LANES]

    y_halves = []
    for hf in range(SSM_HALVES):
        l0 = hf * HALF_LANES
        y_halves.append(
            _dot_nt(h16_ref[:, l0:l0 + HALF_LANES], s5mat_ref[S5_C_RE, hf])
            + _dot_nt(h16_ref[:, PART_LANES + l0:PART_LANES + l0 + HALF_LANES],
                      s5mat_ref[S5_C_IM_NEG, hf]))
    s = jnp.concatenate(y_halves, axis=-1) + d_skip * ub_ref[...]

    gl = jax.nn.gelu(s).astype(BF16)
    branch_b = _dot(gl, w_glu_v_ref[...]) * _sigmoid(_dot(gl, w_glu_g_ref[...]))
    merged = (_sigmoid(gates_ref[:, 0:D_MODEL]) * ya_ref[...]
              + _sigmoid(gates_ref[:, D_MODEL:2 * D_MODEL]) * branch_b).astype(BF16)
    x1 = x + _dot(merged, w_out_ref[...])
    if stream_io:
        x1t_ref[slot] = x1.reshape(t_chunk, n_seq, D_MODEL)
        for c in store_x1(step, slot):
            c.start()

        @pl.when(step == last_step)
        def _drain_out():
            if t_chunk * 2 <= x_ref.shape[1]:
                for c in store_x1(step - 1, 1 - slot):
                    c.wait()
            for c in store_x1(step, slot):
                c.wait()
    else:
        x1_ref[...] = x1.reshape(x1_ref.shape)

    if cast_weights:
        @pl.when(step == pl.num_programs(0) - 1)
        def _finish_exports():
            for i in range(len(big_vmem)):
                export_copy(i).wait()

    for k, (_, _, _, n, _) in enumerate(later_jobs):
        @pl.when(step == k + 1)
        def _convert_chunk(k=k, n=n):
            if k >= 2:
                send(k - 2).wait()
            fetch(k).wait()
            stage16_ref[k % 2, 0:n] = stage_ref[k % 2, 0:n].astype(BF16)
            send(k).start()
            if k + 1 < len(later_jobs):
                fetch(k + 1).start()
    if later_jobs:
        @pl.when(step == len(later_jobs) + 1)
        def _finish_sends():
            for k in range(max(0, len(later_jobs) - 2), len(later_jobs)):
                send(k).wait()


def _mixer_call(x, pool_in, h0, small_w, big_w, later_w=(), *, n_seq, t_chunk, pos_start):
    rows = n_seq * t_chunk
    stream_io = t_chunk > 1
    has_state = pool_in is not None
    cast_weights = big_w[0].dtype == F32
    if stream_io:
        n_steps = x.shape[1] // t_chunk
        x1_shape, x1_spec = x.shape, _IN_HBM
    else:
        n_steps = 1
        x1_shape, x1_spec = (n_seq, D_MODEL), pl.BlockSpec((n_seq, D_MODEL), lambda i: (0, 0))
    keep_rows = POOL_BUF * n_seq

    in_arrays = [x]
    in_specs = [_IN_HBM]
    if has_state:
        in_arrays += [pool_in, h0]
        in_specs += [_resident(pool_in.shape), _resident(h0.shape)]
    assert cast_weights or not later_w
    n_chunks = sum((w.shape[0] // min(CAST_ROWS, w.shape[0])) * (w.shape[1] // CAST_COLS)
                   for w in later_w)
    assert n_chunks + 2 <= n_steps or not later_w, "one chunk of later_w per grid step"
    in_arrays += list(small_w) + list(big_w) + list(later_w)
    in_specs += [_resident(w.shape) for w in small_w]
    in_specs += [_IN_HBM if cast_weights else _resident(w.shape) for w in big_w]
    in_specs += [_IN_HBM for _ in later_w]

    out_shape = [
        jax.ShapeDtypeStruct(x1_shape, F32),
        jax.ShapeDtypeStruct((keep_rows, POOL_WIDTH), F32),
        jax.ShapeDtypeStruct((n_seq, PART_LANES), F32),
        jax.ShapeDtypeStruct((n_seq, PART_LANES), F32),
    ]
    out_specs = [
        x1_spec,
        pl.BlockSpec((keep_rows, POOL_WIDTH), lambda i: (0, 0)),
        pl.BlockSpec((n_seq, PART_LANES), lambda i: (0, 0)),
        pl.BlockSpec((n_seq, PART_LANES), lambda i: (0, 0)),
    ]
    scratch = [
        pltpu.VMEM((rows, 2 * D_MODEL), F32),
        pltpu.VMEM((rows, SSM_WIDTH), F32),
        pltpu.VMEM(((POOL_SLOTS + t_chunk) * n_seq, POOL_WIDTH), F32),
        pltpu.VMEM((rows, STATE_LANES), F32),
        pltpu.VMEM((rows, STATE_LANES), BF16),
        pltpu.VMEM((n_seq, STATE_LANES), F32),
        pltpu.VMEM((rows, D_MODEL), F32),
    ]
    if cast_weights:
        out_shape += [jax.ShapeDtypeStruct(w.shape, BF16) for w in list(big_w) + list(later_w)]
        out_specs += [_IN_HBM for _ in list(big_w) + list(later_w)]
        scratch += [pltpu.VMEM(w.shape, BF16) for w in big_w]
        scratch += [pltpu.VMEM((2, CAST_ROWS, CAST_COLS), F32),
                    pltpu.VMEM((2, CAST_ROWS, CAST_COLS), BF16),
                    pltpu.SemaphoreType.DMA((2,)),
                    pltpu.SemaphoreType.DMA((len(big_w),)),
                    pltpu.SemaphoreType.DMA((2,))]
    if stream_io:
        scratch += [pltpu.VMEM((2, t_chunk, n_seq, D_MODEL), F32),
                    pltpu.VMEM((2, t_chunk, n_seq, D_MODEL), F32),
                    pltpu.SemaphoreType.DMA((2, n_seq)),
                    pltpu.SemaphoreType.DMA((2, n_seq))]
    else:
        scratch += [pltpu.VMEM((n_seq, D_MODEL), F32), pltpu.SemaphoreType.DMA((1,))]
    kern = functools.partial(_mixer_kernel, n_seq=n_seq, t_chunk=t_chunk, stream_io=stream_io,
                             pos_start=pos_start, has_state=has_state, cast_weights=cast_weights,
                             n_later=len(later_w))
    return pl.pallas_call(
        kern,
        grid=(n_steps,),
        in_specs=in_specs,
        out_specs=out_specs,
        out_shape=out_shape,
        scratch_shapes=scratch,
        compiler_params=pltpu.CompilerParams(
            dimension_semantics=("arbitrary",), vmem_limit_bytes=VMEM_LIMIT_BYTES),
        name=f"mixer_n{n_seq}_t{t_chunk}",
    )(*in_arrays)


def _ffn_rows(x1, p, g_ff_ref, g_ple_ref, g_final_ref, w_ff1_ref, w_ff2_ref, w_ple_ref,
              w_ple_gate_ref):
    h2 = _rmsnorm(x1, g_ff_ref[...]).astype(BF16)
    x2 = x1
    for c in range(D_FF // FF_CHUNK):
        hid = _dot(h2, w_ff1_ref[:, c * FF_CHUNK:(c + 1) * FF_CHUNK])
        hid = jnp.square(jnp.maximum(hid, 0.0)).astype(BF16)
        x2 = x2 + _dot(hid, w_ff2_ref[c * FF_CHUNK:(c + 1) * FF_CHUNK, :])
    h3 = _rmsnorm(x2, g_ple_ref[...]).astype(BF16)
    gate = _sigmoid(_dot(h3, w_ple_gate_ref[...]))
    x3 = x2 + _dot(p.astype(BF16), w_ple_ref[...]) * gate
    return _rmsnorm(x3, g_final_ref[...])


def _ffn_kernel(x1_hbm, p_hbm, x1s_ref, ps_hbm, *refs, block_rows):
    params = refs[:7]
    out_hbm, outs_hbm, ps_ref, ys_ref, sem, xb_ref, pb_ref, yb_ref, io_sem = refs[7:]
    step = pl.program_id(0)
    last_step = pl.num_programs(0) - 1
    slot = lax.rem(step, 2)

    def load(i, s):
        rows = pl.ds(i * block_rows, block_rows)
        return [pltpu.make_async_copy(x1_hbm.at[rows, :], xb_ref.at[s], io_sem.at[0, s]),
                pltpu.make_async_copy(p_hbm.at[rows, :], pb_ref.at[s], io_sem.at[1, s])]

    def store(i, s):
        return pltpu.make_async_copy(yb_ref.at[s], out_hbm.at[pl.ds(i * block_rows, block_rows), :],
                                     io_sem.at[2, s])

    @pl.when(step == 0)
    def _first_block():
        for c in load(0, 0):
            c.start()

    @pl.when(step < last_step)
    def _prefetch():
        for c in load(step + 1, 1 - slot):
            c.start()

    for c in load(step, slot):
        c.wait()

    @pl.when(step >= 2)
    def _free_out_slot():
        store(step - 2, slot).wait()

    yb_ref[slot] = _ffn_rows(xb_ref[slot], pb_ref[slot], *params)
    store(step, slot).start()

    @pl.when(step == last_step)
    def _sample_rows():
        p_copy = pltpu.make_async_copy(ps_hbm.at[:, 0, :], ps_ref, sem.at[0])
        p_copy.start()
        p_copy.wait()
        ys_ref[...] = _ffn_rows(x1s_ref[...], ps_ref[...], *params)
        y_copy = pltpu.make_async_copy(ys_ref, outs_hbm.at[:, 0, :], sem.at[1])
        y_copy.start()
        y_copy.wait()
        store(step - 1, 1 - slot).wait()
        store(step, slot).wait()


def _ffn_call(x1, p, x1s, ps, weights, *, block_rows):
    n_rows = x1.shape[0]
    n_s = x1s.shape[0]
    assert n_rows >= 2 * block_rows
    in_specs = [_IN_HBM, _IN_HBM, _resident(x1s.shape), _IN_HBM]
    in_specs += [_resident(w.shape) for w in weights]
    return pl.pallas_call(
        functools.partial(_ffn_kernel, block_rows=block_rows),
        grid=(n_rows // block_rows,),
        in_specs=in_specs,
        out_specs=(_IN_HBM, _IN_HBM),
        out_shape=(jax.ShapeDtypeStruct((n_rows, D_MODEL), F32),
                   jax.ShapeDtypeStruct((n_s, 1, D_MODEL), F32)),
        scratch_shapes=[pltpu.VMEM((n_s, PLE_DIM), F32), pltpu.VMEM((n_s, D_MODEL), F32),
                        pltpu.SemaphoreType.DMA((2,)),
                        pltpu.VMEM((2, block_rows, D_MODEL), F32),
                        pltpu.VMEM((2, block_rows, PLE_DIM), F32),
                        pltpu.VMEM((2, block_rows, D_MODEL), F32),
                        pltpu.SemaphoreType.DMA((3, 2))],
        compiler_params=pltpu.CompilerParams(
            dimension_semantics=("arbitrary",), vmem_limit_bytes=VMEM_LIMIT_BYTES),
        name="ffn",
    )(x1, p, x1s, ps, *weights)


def _s5_params(lam_re, lam_im, log_dt, b_re, b_im, c_re, c_im):
    dt = jnp.exp(log_dt)[:, None]
    mag = jnp.exp(lam_re * dt)
    ang = lam_im * dt
    abar_re = mag * jnp.cos(ang)
    abar_im = mag * jnp.sin(ang)
    den = lam_re * lam_re + lam_im * lam_im
    nr = abar_re - 1.0
    ni = abar_im
    k_re = ((nr * lam_re + ni * lam_im) / den)[:, None, :]
    k_im = ((ni * lam_re - nr * lam_im) / den)[:, None, :]
    b_re_t = jnp.swapaxes(b_re, 1, 2)
    b_im_t = jnp.swapaxes(b_im, 1, 2)
    bbar_re = k_re * b_re_t - k_im * b_im_t
    bbar_im = k_re * b_im_t + k_im * b_re_t
    abar = jnp.concatenate([abar_re.reshape(1, PART_LANES), abar_im.reshape(1, PART_LANES)],
                           axis=1)

    compact = jnp.stack([bbar_re, bbar_im, c_re, c_im]).reshape(4 * SSM_WIDTH, SSM_STATE)
    src = lax.broadcasted_iota(jnp.int32, (SSM_STATE, HALF_LANES), 0)
    dst = lax.broadcasted_iota(jnp.int32, (SSM_STATE, HALF_LANES), 1)
    tiled = jnp.dot(compact, (src == dst % SSM_STATE).astype(F32))
    row = lax.broadcasted_iota(jnp.int32, tiled.shape, 0)
    col_group = lax.broadcasted_iota(jnp.int32, tiled.shape, 1) // SSM_STATE
    keep = (row // SSM_GROUP_DIM) % HALF_GROUPS == col_group
    signed = jnp.where(row >= S5_C_IM_NEG * SSM_WIDTH, -tiled, tiled)
    s5mat = jnp.where(keep, signed, 0.0).astype(BF16).reshape(4, SSM_HALVES, HALF_IN, HALF_LANES)
    return abar, s5mat


def kernel(x_prompt, x_sample, p_prompt, p_sample, state_pool, state_ssm_re, state_ssm_im, g_mix, w_in, w_pool, pool_scale, lam_re, lam_im, log_dt, b_re, b_im, c_re, c_im, d_skip, w_glu_v, w_glu_g, w_out, g_ff, w_ff1, w_ff2, g_ple, w_ple, w_ple_gate, g_final):
    assert w_in.shape[0] == 1, "the final norm is fused into the (single) layer's ffn call"
    batch, seq_len, _ = x_prompt.shape
    dec_batch = x_sample.shape[0]

    abar, s5mat = _s5_params(lam_re[0], lam_im[0], log_dt[0], b_re[0], b_im[0], c_re[0], c_im[0])
    small_w = (g_mix, pool_scale, d_skip, w_pool[0], abar, s5mat)

    x1p, pool_p, re_p, im_p, *w16 = _mixer_call(
        x_prompt, None, None, small_w, (w_in[0], w_glu_v[0], w_glu_g[0], w_out[0]),
        (w_ff1[0], w_ff2[0], w_ple[0], w_ple_gate[0]),
        n_seq=batch, t_chunk=PROMPT_T_CHUNK, pos_start=0)
    mixer_w16, ffn_w16 = w16[:4], w16[4:]
    pool_tm = jnp.swapaxes(state_pool[0], 0, 1).reshape(POOL_BUF * dec_batch, POOL_WIDTH)
    h0 = jnp.concatenate([state_ssm_re.reshape(dec_batch, PART_LANES),
                          state_ssm_im.reshape(dec_batch, PART_LANES)], axis=1)
    x1s, pool_s, re_s, im_s = _mixer_call(
        x_sample, pool_tm, h0, small_w, mixer_w16, n_seq=dec_batch, t_chunk=1, pos_start=PAST_LEN)

    y_prompt, y_sample = _ffn_call(
        x1p.reshape(batch * seq_len, D_MODEL), p_prompt.reshape(batch * seq_len, PLE_DIM),
        x1s, p_sample[0], [g_ff, g_ple, g_final.reshape(1, D_MODEL)] + ffn_w16,
        block_rows=FFN_BLOCK_ROWS)

    def to_buf(tm, n):
        return jnp.swapaxes(tm.reshape(POOL_BUF, n, POOL_WIDTH), 0, 1)[None]

    state_shape = lambda n: (1, n, SSM_GROUPS, SSM_STATE)
    return (y_prompt.reshape(batch, seq_len, D_MODEL), y_sample,
            to_buf(pool_p, batch), to_buf(pool_s, dec_batch),
            re_p.reshape(state_shape(batch)), im_p.reshape(state_shape(batch)),
            re_s.reshape(state_shape(dec_batch)), im_s.reshape(state_shape(dec_batch)))
```
